```python
import math
import jax
import jax.numpy as jnp
from jax import lax
import numpy as np

D_MODEL = 2048
BATCH = 2
SEQ = 8192
DEPTH = 1
DEC_BATCH = 32
DEC_SEQ = 4
PAST_LEN = 16384
PAGE_SIZE = 128

HEAD_DIM = 128
N_ATT_HEADS = (D_MODEL // 2) // HEAD_DIM
N_KV = N_ATT_HEADS // 4
GROUP = N_ATT_HEADS // N_KV
ATT_WIDTH = N_ATT_HEADS * HEAD_DIM
KV_WIDTH = N_KV * HEAD_DIM
N_DN_HEADS = (D_MODEL // 2) // HEAD_DIM
DN_HEAD_DIM = HEAD_DIM
DN_WIDTH = N_DN_HEADS * DN_HEAD_DIM
MIX_WIDTH = ATT_WIDTH + DN_WIDTH
IN_COLS = ATT_WIDTH + 6 * KV_WIDTH + 3 * N_ATT_HEADS + 4 * DN_WIDTH + 2 * N_DN_HEADS
L_CMP = 32
CMP_STRIDE = 16
CMP_RATIO = L_CMP // CMP_STRIDE
CMP_HIDDEN = HEAD_DIM
SEL_BLOCK = 64
N_SEL = 16
WINDOW = 512
Q_BLOCK = 128
NUM_BUCKETS = 32
MAX_DISTANCE = 128
DN_CONV = 4
DN_CHUNK = 64
D_FF = 11 * D_MODEL // 4
FFN_CONV = 3
RMS_EPS = 1e-6
FORCE_SCORE = 1e4
NEG = -1e30
ATT_SCALE = HEAD_DIM ** -0.5

kernel_name = 'hymba_nsa_gdn_convffn_step'


def rms_norm(x, w):
    xf = x.astype(jnp.float32)
    y = xf * lax.rsqrt(jnp.mean(xf * xf, axis=-1, keepdims=True) + RMS_EPS)
    return (y * w.astype(jnp.float32)).astype(x.dtype)


def l2_normalize(x):
    return x * lax.rsqrt(jnp.sum(x * x, axis=-1, keepdims=True) + RMS_EPS)


def rel_bucket(dist):
    n = jnp.maximum(dist, 0)
    exact = NUM_BUCKETS // 2
    nf = jnp.maximum(n, 1).astype(jnp.float32)
    large = exact + (jnp.log(nf / exact) / math.log(MAX_DISTANCE / exact) * (NUM_BUCKETS - exact)).astype(jnp.int32)
    return jnp.where(n < exact, n, jnp.minimum(large, NUM_BUCKETS - 1))


def masked_softmax(s, valid):
    s = jnp.where(valid, s, NEG)
    m = jnp.max(s, axis=-1, keepdims=True)
    p = jnp.where(valid, jnp.exp(s - m), 0.0)
    return p / jnp.maximum(jnp.sum(p, axis=-1, keepdims=True), 1e-30)


def causal_dwconv(x, prefix, w):
    k = w.shape[0]
    t = x.shape[1]
    xp = jnp.concatenate([prefix, x], axis=1)
    y = xp[:, 0:t] * w[0]
    for j in range(1, k):
        y = y + xp[:, j:j + t] * w[j]
    return y, xp[:, xp.shape[1] - (k - 1):]


def compress(x, w1, b1, w2, pe):
    n, seq_len = x.shape[:2]
    nc = (seq_len - L_CMP) // CMP_STRIDE + 1
    nseg = nc + CMP_RATIO - 1
    seg = x[:, :nseg * CMP_STRIDE].reshape(n, nseg, CMP_STRIDE, N_KV, HEAD_DIM)
    pe_r = pe.reshape(CMP_RATIO, CMP_STRIDE, 1, HEAD_DIM)
    h = b1
    for r in range(CMP_RATIO):
        h = h + jnp.einsum('nsjgd,jde->nsge', seg[:, r:r + nc] + pe_r[r], w1[r])
    y = jnp.einsum('nsge,ed->nsgd', jax.nn.silu(h), w2)
    c_end = jnp.arange(nc) * CMP_STRIDE + (L_CMP - 1)
    return y, c_end


def block_overlap(c_end, n_sb):
    c_lo = c_end - (L_CMP - 1)
    c_hi = c_end + 1
    s_lo = jnp.arange(n_sb) * SEL_BLOCK
    s_hi = s_lo + SEL_BLOCK
    ov = jnp.minimum(c_hi[:, None], s_hi[None, :]) - jnp.maximum(c_lo[:, None], s_lo[None, :])
    return (jnp.maximum(ov, 0) / CMP_STRIDE).astype(jnp.float32)


def nsa_cmp_sel(q, q_pos, ck, cv, c_end, ks, vs, rel_bias):
    n, nq = q.shape[:2]
    seq_len = ks.shape[1]
    n_sb = -(-seq_len // SEL_BLOCK)
    k_sel = min(N_SEL, n_sb)
    s = jnp.einsum('nqgrd,ncgd->ngrqc', q, ck, preferred_element_type=jnp.float32) * ATT_SCALE
    dist_c = q_pos[:, None] - c_end[None, :]
    bias_c = rel_bias[rel_bucket(dist_c)].reshape(dist_c.shape + (N_KV, GROUP))
    p_cmp = masked_softmax(s + jnp.transpose(bias_c, (2, 3, 0, 1)), dist_c >= 0)
    o_cmp = jnp.einsum('ngrqc,ncgd->nqgrd', p_cmp.astype(cv.dtype), cv)
    imp = jnp.einsum('ngrqc,cs->ngqs', p_cmp, block_overlap(c_end, n_sb))
    blk = jnp.arange(n_sb)[None, :]
    cur = (q_pos // SEL_BLOCK)[:, None]
    causal = blk * SEL_BLOCK <= q_pos[:, None]
    forced = (blk == 0) | (blk == cur) | (blk == cur - 1)
    score = jnp.where(causal, imp + jnp.where(forced, FORCE_SCORE, 0.0), NEG)
    _, sel = lax.top_k(score, k_sel)
    tok = (sel[..., None] * SEL_BLOCK + jnp.arange(SEL_BLOCK)).reshape(n, N_KV, nq, k_sel * SEL_BLOCK)
    valid = tok <= q_pos[:, None]
    tok_c = jnp.minimum(tok, seq_len - 1)
    take = jax.vmap(jax.vmap(lambda rows, idx: rows[idx]))
    kg = take(jnp.swapaxes(ks, 1, 2), tok_c)
    vg = take(jnp.swapaxes(vs, 1, 2), tok_c)
    s2 = jnp.einsum('nqgrd,ngqkd->ngrqk', q, kg, preferred_element_type=jnp.float32) * ATT_SCALE
    bias_s = rel_bias.reshape(NUM_BUCKETS, N_KV, GROUP)[rel_bucket(q_pos[:, None] - tok), jnp.arange(N_KV)[None, :, None, None]]
    p_sel = masked_softmax(s2 + jnp.moveaxis(bias_s, -1, 2), valid[:, :, None])
    o_sel = jnp.einsum('ngrqk,ngqkd->nqgrd', p_sel.astype(vg.dtype), vg)
    return o_cmp, o_sel


def window_attend(q, k, v, q_pos, k_pos, rel_bias):
    dist = q_pos[:, :, None] - k_pos[:, None, :]
    valid = (dist >= 0) & (dist < WINDOW) & (k_pos[:, None, :] >= 0)
    s = jnp.einsum('nbqgrd,nbkgd->nbgrqk', q, k, preferred_element_type=jnp.float32) * ATT_SCALE
    bias = rel_bias[rel_bucket(dist)].reshape(dist.shape + (N_KV, GROUP))
    p = masked_softmax(s + jnp.transpose(bias, (0, 3, 4, 1, 2)), valid[:, None, None])
    return jnp.einsum('nbgrqk,nbkgd->nbqgrd', p.astype(v.dtype), v)


def window_prompt(q, kw, vw, rel_bias):
    n, t = q.shape[:2]
    nb = t // Q_BLOCK
    nw = WINDOW // Q_BLOCK
    def band(a):
        pad = jnp.zeros((n, WINDOW) + a.shape[2:], a.dtype)
        ap = jnp.concatenate([pad, a], axis=1).reshape(n, nb + nw, Q_BLOCK, N_KV, HEAD_DIM)
        return jnp.concatenate([ap[:, i:i + nb] for i in range(nw + 1)], axis=2)
    q_pos = jnp.arange(t).reshape(nb, Q_BLOCK)
    k_pos = jnp.arange(nb)[:, None] * Q_BLOCK - WINDOW + jnp.arange((nw + 1) * Q_BLOCK)[None, :]
    o = window_attend(q.reshape(n, nb, Q_BLOCK, N_KV, GROUP, HEAD_DIM), band(kw), band(vw), q_pos, k_pos, rel_bias)
    return o.reshape(q.shape)


def chunk_gated_delta(q, k, v, g, beta, s0):
    n, t, nh, dk = q.shape
    dv = v.shape[-1]
    c = DN_CHUNK
    nck = -(-t // c)
    pad = nck * c - t
    def prep(a):
        a = jnp.pad(a, [(0, 0), (0, pad)] + [(0, 0)] * (a.ndim - 2))
        return jnp.moveaxis(a.reshape((n, nck, c) + a.shape[2:]), 3, 1)
    q, k, v, g, beta = prep(q), prep(k), prep(v), prep(g), prep(beta)
    gc = jnp.cumsum(g, axis=-1)
    tril = jnp.tril(jnp.ones((c, c), bool))
    strict = jnp.tril(jnp.ones((c, c), bool), -1)
    decay = jnp.exp(jnp.where(tril, gc[..., :, None] - gc[..., None, :], NEG))
    kb = k * beta[..., None]
    a_mat = jnp.where(strict, jnp.einsum('nhcid,nhcjd->nhcij', kb, k) * decay, 0.0)
    rhs = jnp.concatenate([v * beta[..., None], kb * jnp.exp(gc)[..., None]], axis=-1)
    sol = lax.linalg.triangular_solve(a_mat, rhs, left_side=True, lower=True, unit_diagonal=True)
    u, w = sol[..., :dv], sol[..., dv:]
    attn = jnp.einsum('nhcid,nhcjd->nhcij', q, k) * decay
    qg = q * jnp.exp(gc)[..., None]
    kd = k * jnp.exp(gc[..., -1:] - gc)[..., None]
    g_last = jnp.exp(gc[..., -1])
    def step(s, xs):
        u_i, w_i, attn_i, qg_i, kd_i, gl_i = xs
        v_new = u_i - jnp.einsum('nhcd,nhde->nhce', w_i, s)
        o_i = jnp.einsum('nhcd,nhde->nhce', qg_i, s) + jnp.einsum('nhij,nhje->nhie', attn_i, v_new)
        s = s * gl_i[..., None, None] + jnp.einsum('nhcd,nhce->nhde', kd_i, v_new)
        return s, o_i
    xs = tuple(jnp.moveaxis(a, 2, 0) for a in (u, w, attn, qg, kd, g_last))
    s_final, o = lax.scan(step, s0, xs)
    o = jnp.transpose(o, (1, 0, 3, 2, 4)).reshape(n, nck * c, nh, dv)[:, :t]
    return o, s_final


def gated_deltanet(qkv, z, a, b, conv_prev, s_prev, conv_w, a_log, dt_bias, norm_w):
    n, t, _ = qkv.shape
    qkv_c, conv_new = causal_dwconv(qkv, conv_prev, conv_w)
    qkv_c = jax.nn.silu(qkv_c.astype(jnp.float32))
    q, k, v = [p.reshape(n, t, N_DN_HEADS, DN_HEAD_DIM) for p in jnp.split(qkv_c, 3, axis=-1)]
    q = l2_normalize(q) * DN_HEAD_DIM ** -0.5
    k = l2_normalize(k)
    beta = jax.nn.sigmoid(b.astype(jnp.float32))
    g = -jnp.exp(a_log.astype(jnp.float32)) * jax.nn.softplus(a.astype(jnp.float32) + dt_bias.astype(jnp.float32))
    o, s_new = chunk_gated_delta(q, k, v, g, beta, s_prev.astype(jnp.float32))
    o = o * lax.rsqrt(jnp.mean(o * o, axis=-1, keepdims=True) + RMS_EPS) * norm_w.astype(jnp.float32)
    o = o * jax.nn.silu(z.astype(jnp.float32).reshape(n, t, N_DN_HEADS, DN_HEAD_DIM))
    return o.reshape(n, t, DN_WIDTH).astype(qkv.dtype), s_new.astype(s_prev.dtype), conv_new


def split_points():
    sizes = (ATT_WIDTH, KV_WIDTH, KV_WIDTH, KV_WIDTH, KV_WIDTH, KV_WIDTH, KV_WIDTH,
             3 * N_ATT_HEADS, 3 * DN_WIDTH, DN_WIDTH, N_DN_HEADS, N_DN_HEADS)
    return [int(c) for c in np.cumsum(sizes)[:-1]]


def gather_pages(cache, page_table):
    rows = cache[page_table]
    return rows.reshape(rows.shape[0], rows.shape[1] * rows.shape[2], N_KV, HEAD_DIM)


def trunk_layer(x, attend, dn_conv_prev, dn_state_prev, ffn_conv_prev, lw):
    (w_in, w_out, n_pre_mix, n_post_mix, n_pre_ffn, n_post_ffn,
     dn_conv_w, dn_a_log, dn_dt_bias, dn_norm_w, w_up, ffn_conv_w, w_down) = lw
    n, t, _ = x.shape
    h = rms_norm(x, n_pre_mix)
    (q, kc, vc, ks, vs, kw, vw, gates, qkv_dn, z_dn, a_dn, b_dn) = jnp.split(h @ w_in, split_points(), axis=-1)
    heads_kv = lambda a: a.reshape(n, t, N_KV, HEAD_DIM)
    q = q.reshape(n, t, N_KV, GROUP, HEAD_DIM)
    o_cmp, o_sel, o_win, nsa_state = attend(q, heads_kv(kc), heads_kv(vc), heads_kv(ks), heads_kv(vs), heads_kv(kw), heads_kv(vw))
    gt = jax.nn.sigmoid(gates.astype(jnp.float32)).reshape(n, t, N_KV, GROUP, 3, 1)
    o_att = (gt[..., 0, :] * o_cmp + gt[..., 1, :] * o_sel + gt[..., 2, :] * o_win).astype(x.dtype).reshape(n, t, ATT_WIDTH)
    o_dn, dn_state, dn_conv = gated_deltanet(qkv_dn, z_dn, a_dn, b_dn, dn_conv_prev, dn_state_prev,
                                             dn_conv_w, dn_a_log, dn_dt_bias, dn_norm_w)
    mix = jnp.concatenate([o_att, o_dn], axis=-1) @ w_out
    x = x + rms_norm(mix, n_post_mix)
    up, ffn_conv = causal_dwconv(rms_norm(x, n_pre_ffn) @ w_up, ffn_conv_prev, ffn_conv_w)
    gate, val = jnp.split(up, 2, axis=-1)
    x = x + rms_norm((jax.nn.silu(gate) * val) @ w_down, n_post_ffn)
    return x, nsa_state, dn_state, dn_conv, ffn_conv


def setup_inputs(seed: int = 0) -> dict:
    keys = iter(jax.random.split(jax.random.key(seed), 48))
    def nrm(shape, scale):
        return jax.random.normal(next(keys), shape, jnp.float32) * scale
    def gain(shape):
        return 1.0 + nrm(shape, 0.02)
    n_pages = PAST_LEN // PAGE_SIZE
    used = DEC_BATCH * n_pages
    pool = used + max(1, used // 4)
    win_keep = min(WINDOW, PAST_LEN)
    paged = (DEPTH, pool, PAGE_SIZE, N_KV, HEAD_DIM)
    page_table = jax.random.permutation(next(keys), pool)[:used].reshape(DEC_BATCH, n_pages).astype(jnp.int32)
    dt = jnp.exp(jax.random.uniform(next(keys), (DEPTH, N_DN_HEADS), jnp.float32, math.log(1e-3), math.log(1e-1)))
    a_log = jnp.log(jax.random.uniform(next(keys), (DEPTH, N_DN_HEADS), jnp.float32, 1.0, 16.0))
    w1_shape = (DEPTH, CMP_RATIO, CMP_STRIDE, HEAD_DIM, CMP_HIDDEN)
    return {
        'x_prompt': nrm((BATCH, SEQ, D_MODEL), 1.0),
        'x_sample': nrm((DEC_BATCH, DEC_SEQ, D_MODEL), 1.0),
        'cache_cmp_k': nrm(paged, 1.0),
        'cache_cmp_v': nrm(paged, 1.0),
        'cache_sel_k': nrm(paged, 1.0),
        'cache_sel_v': nrm(paged, 1.0),
        'cache_win_k': nrm((DEPTH, DEC_BATCH, win_keep, N_KV, HEAD_DIM), 1.0),
        'cache_win_v': nrm((DEPTH, DEC_BATCH, win_keep, N_KV, HEAD_DIM), 1.0),
        'state_dn': nrm((DEPTH, DEC_BATCH, N_DN_HEADS, DN_HEAD_DIM, DN_HEAD_DIM), 0.05),
        'state_dn_conv': nrm((DEPTH, DEC_BATCH, DN_CONV - 1, 3 * DN_WIDTH), 1.0),
        'state_ffn_conv': nrm((DEPTH, DEC_BATCH, FFN_CONV - 1, 2 * D_FF), 1.0),
        'page_table': page_table,
        'rel_bias': nrm((NUM_BUCKETS, N_ATT_HEADS), 0.5),
        'w_in': nrm((DEPTH, D_MODEL, IN_COLS), D_MODEL ** -0.5),
        'w_out': nrm((DEPTH, MIX_WIDTH, D_MODEL), MIX_WIDTH ** -0.5),
        'norm_pre_mix': gain((DEPTH, D_MODEL)),
        'norm_post_mix': gain((DEPTH, D_MODEL)),
        'norm_pre_ffn': gain((DEPTH, D_MODEL)),
        'norm_post_ffn': gain((DEPTH, D_MODEL)),
        'cmp_w1_k': nrm(w1_shape, (L_CMP * HEAD_DIM) ** -0.5),
        'cmp_b1_k': nrm((DEPTH, CMP_HIDDEN), 0.01),
        'cmp_w2_k': nrm((DEPTH, CMP_HIDDEN, HEAD_DIM), CMP_HIDDEN ** -0.5),
        'cmp_pe_k': nrm((DEPTH, L_CMP, HEAD_DIM), 0.1),
        'cmp_w1_v': nrm(w1_shape, (L_CMP * HEAD_DIM) ** -0.5),
        'cmp_b1_v': nrm((DEPTH, CMP_HIDDEN), 0.01),
        'cmp_w2_v': nrm((DEPTH, CMP_HIDDEN, HEAD_DIM), CMP_HIDDEN ** -0.5),
        'cmp_pe_v': nrm((DEPTH, L_CMP, HEAD_DIM), 0.1),
        'dn_conv_w': nrm((DEPTH, DN_CONV, 3 * DN_WIDTH), DN_CONV ** -0.5),
        'dn_a_log': a_log,
        'dn_dt_bias': dt + jnp.log(-jnp.expm1(-dt)),
        'dn_norm_w': gain((DEPTH, DN_HEAD_DIM)),
        'ffn_w_up': nrm((DEPTH, D_MODEL, 2 * D_FF), D_MODEL ** -0.5),
        'ffn_conv_w': nrm((DEPTH, FFN_CONV, 2 * D_FF), FFN_CONV ** -0.5),
        'ffn_w_down': nrm((DEPTH, D_FF, D_MODEL), D_FF ** -0.5),
    }


def reference(x_prompt, x_sample, cache_cmp_k, cache_cmp_v, cache_sel_k, cache_sel_v, cache_win_k, cache_win_v,
              state_dn, state_dn_conv, state_ffn_conv, page_table, rel_bias, w_in, w_out,
              norm_pre_mix, norm_post_mix, norm_pre_ffn, norm_post_ffn,
              cmp_w1_k, cmp_b1_k, cmp_w2_k, cmp_pe_k, cmp_w1_v, cmp_b1_v, cmp_w2_v, cmp_pe_v,
              dn_conv_w, dn_a_log, dn_dt_bias, dn_norm_w, ffn_w_up, ffn_conv_w, ffn_w_down):
    yp, ys = x_prompt, x_sample
    nb_p, t_p = x_prompt.shape[:2]
    nb_s, t_s = x_sample.shape[:2]
    keep_p = min(WINDOW, t_p)
    keep_s = min(WINDOW, PAST_LEN + t_s)
    wk_s = cache_win_k.shape[2]
    pst = [[] for _ in range(9)]
    sst = [[] for _ in range(9)]
    for l in range(DEPTH):
        lw = (w_in[l], w_out[l], norm_pre_mix[l], norm_post_mix[l], norm_pre_ffn[l], norm_post_ffn[l],
              dn_conv_w[l], dn_a_log[l], dn_dt_bias[l], dn_norm_w[l], ffn_w_up[l], ffn_conv_w[l], ffn_w_down[l])
        phi_k = (cmp_w1_k[l], cmp_b1_k[l], cmp_w2_k[l], cmp_pe_k[l])
        phi_v = (cmp_w1_v[l], cmp_b1_v[l], cmp_w2_v[l], cmp_pe_v[l])

        def attend_prompt(q, kc, vc, ks, vs, kw, vw):
            ck, c_end = compress(kc, *phi_k)
            cv, _ = compress(vc, *phi_v)
            nqb = t_p // Q_BLOCK
            qb = jnp.swapaxes(q.reshape(nb_p, nqb, Q_BLOCK, N_KV, GROUP, HEAD_DIM), 0, 1)
            qpos = jnp.arange(t_p).reshape(nqb, Q_BLOCK)
            o_cmp, o_sel = lax.map(lambda a: nsa_cmp_sel(a[0], a[1], ck, cv, c_end, ks, vs, rel_bias), (qb, qpos))
            unblock = lambda o: jnp.swapaxes(o, 0, 1).reshape(q.shape)
            o_win = window_prompt(q, kw, vw, rel_bias)
            return unblock(o_cmp), unblock(o_sel), o_win, (kc, vc, ks, vs, kw[:, t_p - keep_p:], vw[:, t_p - keep_p:])

        def attend_sample(q, kc, vc, ks, vs, kw, vw):
            kc_all = jnp.concatenate([gather_pages(cache_cmp_k[l], page_table), kc], axis=1)
            vc_all = jnp.concatenate([gather_pages(cache_cmp_v[l], page_table), vc], axis=1)
            ks_all = jnp.concatenate([gather_pages(cache_sel_k[l], page_table), ks], axis=1)
            vs_all = jnp.concatenate([gather_pages(cache_sel_v[l], page_table), vs], axis=1)
            ck, c_end = compress(kc_all, *phi_k)
            cv, _ = compress(vc_all, *phi_v)
            q_pos = PAST_LEN + jnp.arange(t_s)
            o_cmp, o_sel = nsa_cmp_sel(q, q_pos, ck, cv, c_end, ks_all, vs_all, rel_bias)
            kw_all = jnp.concatenate([cache_win_k[l], kw], axis=1)
            vw_all = jnp.concatenate([cache_win_v[l], vw], axis=1)
            k_pos = PAST_LEN - wk_s + jnp.arange(wk_s + t_s)
            o_win = window_attend(q[:, None], kw_all[:, None], vw_all[:, None], q_pos[None], k_pos[None], rel_bias)[:, 0]
            n_all = wk_s + t_s
            return o_cmp, o_sel, o_win, (kc, vc, ks, vs, kw_all[:, n_all - keep_s:], vw_all[:, n_all - keep_s:])

        yp, nsa_p, dn_s_p, dn_c_p, ffn_c_p = trunk_layer(
            yp, attend_prompt,
            jnp.zeros((nb_p, DN_CONV - 1, 3 * DN_WIDTH), yp.dtype),
            jnp.zeros((nb_p, N_DN_HEADS, DN_HEAD_DIM, DN_HEAD_DIM), yp.dtype),
            jnp.zeros((nb_p, FFN_CONV - 1, 2 * D_FF), yp.dtype), lw)
        ys, nsa_s, dn_s_s, dn_c_s, ffn_c_s = trunk_layer(
            ys, attend_sample, state_dn_conv[l], state_dn[l], state_ffn_conv[l], lw)
        for lst, val in zip(pst, nsa_p + (dn_s_p, dn_c_p, ffn_c_p)):
            lst.append(val)
        for lst, val in zip(sst, nsa_s + (dn_s_s, dn_c_s, ffn_c_s)):
            lst.append(val)
    (p_cmp_k, p_cmp_v, p_sel_k, p_sel_v, p_win_k, p_win_v, p_dn, p_dn_conv, p_ffn_conv) = [jnp.stack(a) for a in pst]
    (s_cmp_k, s_cmp_v, s_sel_k, s_sel_v, s_win_k, s_win_v, s_dn, s_dn_conv, s_ffn_conv) = [jnp.stack(a) for a in sst]
    return (yp, ys,
            p_cmp_k, p_cmp_v, p_sel_k, p_sel_v, p_win_k, p_win_v, p_dn, p_dn_conv, p_ffn_conv,
            s_cmp_k, s_cmp_v, s_sel_k, s_sel_v, s_win_k, s_win_v, s_dn, s_dn_conv, s_ffn_conv)
```

```python
import functools
import math

import numpy as np
import jax
import jax.numpy as jnp
from jax import lax
from jax.experimental import pallas as pl
from jax.experimental.pallas import tpu as pltpu

F32 = jnp.float32
BF16 = jnp.bfloat16
HI = lax.Precision.HIGHEST

LANES = 128
SUBLANES = 8
VMEM_LIMIT = 56 * 1024 * 1024

HEAD_DIM = 128
N_KV = 2
GROUP = 4
N_HEADS = N_KV * GROUP
ATT_W = N_HEADS * HEAD_DIM
KV_W = N_KV * HEAD_DIM
N_DN = 8
DN_W = N_DN * HEAD_DIM
L_CMP = 32
CMP_STRIDE = 16
SEL_BLOCK = 64
N_SEL = 16
WINDOW = 512
Q_BLOCK = 128
NUM_BUCKETS = 32
DN_CONV = 4
DN_CHUNK = 64
FFN_CONV = 3
PAGE = 128
RMS_EPS = 1e-6
FORCE_SCORE = 1e4
NEG = -1e30
ATT_SCALE = HEAD_DIM ** -0.5

C_DQ, C_DK, C_DV = 0, DN_W, 2 * DN_W
C_Z = 3 * DN_W
C_Q = 4 * DN_W
C_KC = C_Q + ATT_W
C_VC, C_KS, C_VS, C_KW, C_VW = (C_KC + KV_W * i for i in range(1, 6))
C_SM = C_KC + 6 * KV_W
SM_GATE, SM_A, SM_B = 0, 3 * N_HEADS, 3 * N_HEADS + N_DN
PROJ_TN = 512
NP_COLS = -(-(C_SM + LANES) // PROJ_TN) * PROJ_TN
BF_W = C_SM - C_Q
B_Q, B_KC, B_VC, B_KS, B_VS, B_KW, B_VW = (c - C_Q for c in (C_Q, C_KC, C_VC, C_KS, C_VS, C_KW, C_VW))

CK_FRONT = 120
CK_BACK = 8
CT_OFF = CMP_STRIDE * CK_FRONT - (L_CMP - 1)


def _bucket_lower_bounds():
    n = np.arange(0, 1024)
    nf = np.maximum(n, 1).astype(np.float32)
    exact = NUM_BUCKETS // 2
    large = exact + (np.log(nf / np.float32(exact)) / np.float32(math.log(128 / exact))
                     * np.float32(NUM_BUCKETS - exact)).astype(np.int32)
    b = np.where(n < exact, n, np.minimum(large, NUM_BUCKETS - 1))
    return tuple(int(n[b >= k].min()) for k in range(NUM_BUCKETS))


BUCKET_LO = _bucket_lower_bounds()
CONST_BUCKET_DIST = BUCKET_LO[-1]


def _cp(*sem):
    return pltpu.CompilerParams(dimension_semantics=sem, vmem_limit_bytes=VMEM_LIMIT)


def _dot(a, b, precision=None):
    return jnp.dot(a, b, preferred_element_type=F32, precision=precision)


def _dot_nt(a, b, precision=None):
    return lax.dot_general(a, b, (((1,), (1,)), ((), ())), preferred_element_type=F32, precision=precision)


def _dot_tn(a, b, precision=None):
    return lax.dot_general(a, b, (((0,), (0,)), ((), ())), preferred_element_type=F32, precision=precision)


def _dot_split(a, b_bf):
    hi = a.astype(BF16)
    lo = (a - hi.astype(F32)).astype(BF16)
    return _dot(hi, b_bf) + _dot(lo, b_bf)


def _iota(shape, dim):
    return lax.broadcasted_iota(jnp.int32, shape, dim)


def _silu(x):
    return x * (1.0 / (1.0 + jnp.exp(-x)))


def _sigmoid(x):
    return 1.0 / (1.0 + jnp.exp(-x))


def _rms(x, w):
    return x * lax.rsqrt(jnp.mean(x * x, axis=-1, keepdims=True) + RMS_EPS) * w


def _rel_bias(dist, rb_of):
    out = jnp.zeros(dist.shape, F32) + rb_of(0)
    for k in range(1, NUM_BUCKETS):
        out = jnp.where(dist >= BUCKET_LO[k], rb_of(k), out)
    return out


def _inproj_kernel(x_ref, nw_ref, w_ref, o_ref, ob_ref, hn_ref, *, jb0, jb1):
    j = pl.program_id(1)

    @pl.when(j == 0)
    def _():
        hn_ref[...] = _rms(x_ref[...], nw_ref[...]).astype(BF16)

    acc = _dot(hn_ref[...], w_ref[...])
    o_ref[...] = acc

    @pl.when((j >= jb0) & (j < jb1))
    def _():
        ob_ref[...] = acc.astype(BF16)


def _in_proj(x2d, nw, w_bf):
    r, d = x2d.shape
    tm = min(512, r)
    tn = PROJ_TN
    jb0, nb = C_Q // tn, BF_W // tn
    return pl.pallas_call(
        functools.partial(_inproj_kernel, jb0=jb0, jb1=jb0 + nb),
        grid=(r // tm, NP_COLS // tn),
        in_specs=[pl.BlockSpec((tm, d), lambda i, j: (i, 0)),
                  pl.BlockSpec((1, d), lambda i, j: (0, 0)),
                  pl.BlockSpec((d, tn), lambda i, j: (0, j))],
        out_specs=[pl.BlockSpec((tm, tn), lambda i, j: (i, j)),
                   pl.BlockSpec((tm, tn), lambda i, j: (i, jnp.clip(j - jb0, 0, nb - 1)))],
        out_shape=[jax.ShapeDtypeStruct((r, NP_COLS), F32), jax.ShapeDtypeStruct((r, BF_W), BF16)],
        scratch_shapes=[pltpu.VMEM((tm, d), BF16)],
        compiler_params=_cp("parallel", "arbitrary"),
        name="in_proj",
    )(x2d, nw, w_bf)


def _cmp_seg_kernel(*refs, n_src, segs):
    row_refs, w_ref, o_ref = refs[-(N_KV * n_src + 2):-2], refs[-2], refs[-1]
    for g in range(N_KV):
        parts = []
        for r in row_refs[g * n_src:(g + 1) * n_src]:
            cols = [r[pl.ds(j, segs, stride=CMP_STRIDE), :] for j in range(CMP_STRIDE)]
            parts.append(jnp.concatenate(cols, axis=1))
        a = parts[0] if n_src == 1 else jnp.concatenate(parts, axis=0)
        o_ref[:, g * 2 * HEAD_DIM:(g + 1) * 2 * HEAD_DIM] = _dot(a.astype(BF16), w_ref[...])


def _cmp_seg_prompt(proj3, col, w_cat):
    n, t, _ = proj3.shape
    rows = min(2048, t)
    segs = rows // CMP_STRIDE
    return pl.pallas_call(
        functools.partial(_cmp_seg_kernel, n_src=1, segs=segs),
        grid=(n, t // rows),
        in_specs=[pl.BlockSpec((None, rows, HEAD_DIM), lambda i, c, g=g: (i, c, col // HEAD_DIM + g))
                  for g in range(N_KV)]
                 + [pl.BlockSpec(w_cat.shape, lambda i, c: (0, 0))],
        out_specs=pl.BlockSpec((None, segs, 2 * KV_W), lambda i, c: (i, c, 0)),
        out_shape=jax.ShapeDtypeStruct((n, t // CMP_STRIDE, 2 * KV_W), F32),
        compiler_params=_cp("parallel", "parallel"),
        name="cmp_seg_prompt",
    )(proj3, proj3, w_cat)


PAGES_PER_STEP = 8


def _cmp_seg_pages(cache, page_table, w_cat):
    ns, n_pages = page_table.shape
    segs = PAGE // CMP_STRIDE

    def page_spec(k, g):
        return pl.BlockSpec((None, PAGE, HEAD_DIM), lambda i, s, pt: (pt[i, s * PAGES_PER_STEP + k], 0, g))

    grid_spec = pltpu.PrefetchScalarGridSpec(
        num_scalar_prefetch=1,
        grid=(ns, n_pages // PAGES_PER_STEP),
        in_specs=[page_spec(k, g) for g in range(N_KV) for k in range(PAGES_PER_STEP)]
                 + [pl.BlockSpec(w_cat.shape, lambda i, s, pt: (0, 0))],
        out_specs=pl.BlockSpec((None, PAGES_PER_STEP * segs, 2 * KV_W), lambda i, s, pt: (i, s, 0)),
    )
    return pl.pallas_call(
        functools.partial(_cmp_seg_kernel, n_src=PAGES_PER_STEP, segs=segs),
        grid_spec=grid_spec,
        out_shape=jax.ShapeDtypeStruct((ns, n_pages * segs, 2 * KV_W), F32),
        compiler_params=_cp("parallel", "parallel"),
        name="cmp_seg_pages",
    )(page_table, *([cache] * (N_KV * PAGES_PER_STEP)), w_cat)


def _cmp_mlp_kernel(p_ref, pe_ref, w1_ref, b1_ref, w2_ref, o_ref, *, nseg, front, back):
    p = p_ref[...]
    p0 = p[:, :HEAD_DIM]
    p1 = pltpu.roll(p[:, HEAD_DIM:], nseg - 1, 0)
    bias = b1_ref[...]
    for r in range(L_CMP // CMP_STRIDE):
        pe = jnp.broadcast_to(pe_ref[r:r + 1, :], (SUBLANES, pe_ref.shape[1]))
        bias = bias + _dot(pe, w1_ref[r], HI)[0:1, :]
    h = p0 + p1 + bias
    y = _dot(_silu(h).astype(BF16), w2_ref[...])
    y = jnp.where(_iota(y.shape, 0) < nseg - 1, y, 0.0)
    if front:
        o_ref[0:front, :] = jnp.zeros((front, HEAD_DIM), F32)
    o_ref[front:front + nseg, :] = y
    if back:
        o_ref[front + nseg:front + nseg + back, :] = jnp.zeros((back, HEAD_DIM), F32)


def _cmp_mlp(p, pe2, w1f, b1, w2_bf, front, back):
    n, nseg, _ = p.shape
    rows = front + nseg + back
    return pl.pallas_call(
        functools.partial(_cmp_mlp_kernel, nseg=nseg, front=front, back=back),
        grid=(n, N_KV),
        in_specs=[pl.BlockSpec((None, nseg, 2 * HEAD_DIM), lambda i, g: (i, 0, g)),
                  pl.BlockSpec(pe2.shape, lambda i, g: (0, 0)),
                  pl.BlockSpec(w1f.shape, lambda i, g: (0, 0, 0)),
                  pl.BlockSpec(b1.shape, lambda i, g: (0, 0)),
                  pl.BlockSpec(w2_bf.shape, lambda i, g: (0, 0))],
        out_specs=pl.BlockSpec((None, None, rows, HEAD_DIM), lambda i, g: (i, g, 0, 0)),
        out_shape=jax.ShapeDtypeStruct((n, N_KV, rows, HEAD_DIM), F32),
        compiler_params=_cp("parallel", "parallel"),
        name="cmp_mlp",
    )(p, pe2, w1f, b1, w2_bf)


def _bias_tiles_kernel(rb_ref, bt_ref, ct_ref):
    h = pl.program_id(0)
    shape = (Q_BLOCK, Q_BLOCK)
    i, j = _iota(shape, 0), _iota(shape, 1)
    rb_of = lambda k: rb_ref[k, h]
    for m in range(2):
        bt_ref[m] = _rel_bias(i - j + Q_BLOCK * m, rb_of)
    ct_ref[...] = _rel_bias(i - CMP_STRIDE * j + CT_OFF, rb_of)


def _bias_tiles(rel_bias):
    return pl.pallas_call(
        _bias_tiles_kernel,
        grid=(N_HEADS,),
        in_specs=[pl.BlockSpec(memory_space=pltpu.SMEM)],
        out_specs=[pl.BlockSpec((None, 2, Q_BLOCK, Q_BLOCK), lambda h: (h, 0, 0, 0)),
                   pl.BlockSpec((None, Q_BLOCK, Q_BLOCK), lambda h: (h, 0, 0))],
        out_shape=[jax.ShapeDtypeStruct((N_HEADS, 2, Q_BLOCK, Q_BLOCK), F32),
                   jax.ShapeDtypeStruct((N_HEADS, Q_BLOCK, Q_BLOCK), F32)],
        compiler_params=_cp("parallel"),
        name="bias_tiles",
    )(rel_bias)


def _stack_heads(q_ref):
    return jnp.concatenate([q_ref[:, r * HEAD_DIM:(r + 1) * HEAD_DIM] for r in range(GROUP)], axis=0)


def _per_head_rows(vals, rows):
    return jnp.concatenate([jnp.full((rows, 1), v, F32) for v in vals], axis=0)


def _online_update(s, v_bf, m_ref, l_ref, acc_ref, valid=None):
    m_old = m_ref[...]
    m_new = jnp.maximum(m_old, jnp.max(s, axis=1, keepdims=True))
    alpha = jnp.exp(m_old - m_new)
    e = jnp.exp(s - m_new)
    if valid is not None:
        e = jnp.where(valid, e, 0.0)
    l_ref[...] = alpha * l_ref[...] + jnp.sum(e, axis=1, keepdims=True)
    acc_ref[...] = alpha * acc_ref[...] + _dot(e.astype(BF16), v_bf)
    m_ref[...] = m_new
    return e, alpha


def _top_k_mask(score):
    lane = _iota(score.shape, 1)
    sel = jnp.zeros(score.shape, F32)
    for _ in range(N_SEL):
        mx = jnp.max(score, axis=1, keepdims=True)
        idx = jnp.min(jnp.where(score == mx, lane, score.shape[1]), axis=1, keepdims=True)
        hit = lane == idx
        sel = jnp.where(hit, 1.0, sel)
        score = jnp.where(hit, -jnp.inf, score)
    return sel


def _block_scores(imp, qpos):
    blk = _iota(imp.shape, 1)
    cur = qpos >> 6
    causal = blk * SEL_BLOCK <= qpos
    forced = (blk == 0) | (blk == cur) | (blk == cur - 1)
    return jnp.where(causal, imp + jnp.where(forced, FORCE_SCORE, 0.0), NEG)


def _overlap_tile(c0, shape):
    c = c0 + _iota(shape, 0)
    s = _iota(shape, 1)
    return ((s == (c >> 2)).astype(F32) + (s == ((c + 1) >> 2)).astype(F32)).astype(BF16)


def _pcmp_kernel(rb_ref, q_ref, ck_ref, cv_ref, ct_ref, o_ref, sel_ref, m_ref, l_ref, acc_ref, imp_ref, *, n_sb):
    g, b = pl.program_id(1), pl.program_id(2)
    rows = GROUP * Q_BLOCK
    q = _stack_heads(q_ref)
    m_ref[...] = jnp.full((rows, 1), NEG, F32)
    l_ref[...] = jnp.zeros((rows, 1), F32)
    acc_ref[...] = jnp.zeros((rows, HEAD_DIM), F32)
    imp_ref[...] = jnp.zeros((rows, n_sb), F32)
    rb_last = _per_head_rows([rb_ref[NUM_BUCKETS - 1, g * GROUP + r] for r in range(GROUP)], Q_BLOCK)
    n_old_blocks = SUBLANES * b - CK_FRONT

    def step(c0, row0, bias, valid):
        kt = ck_ref[pl.ds(row0, Q_BLOCK), :].astype(BF16)
        vt = cv_ref[pl.ds(row0, Q_BLOCK), :].astype(BF16)
        s = jnp.where(valid, _dot_nt(q, kt) * ATT_SCALE + bias, NEG)
        e, alpha = _online_update(s, vt, m_ref, l_ref, acc_ref, valid)
        imp_ref[...] = alpha * imp_ref[...] + _dot_split(e, _overlap_tile(c0, (Q_BLOCK, n_sb)))

    def old_tile(t, carry):
        c0 = t * Q_BLOCK
        valid = (c0 + _iota((rows, Q_BLOCK), 1)) < n_old_blocks
        step(c0, pl.multiple_of(CK_FRONT + c0, SUBLANES), rb_last, valid)
        return carry

    lax.fori_loop(0, (jnp.maximum(n_old_blocks, 0) + Q_BLOCK - 1) // Q_BLOCK, old_tile, 0)

    shape = (rows, Q_BLOCK)
    i = _iota(shape, 0) & (Q_BLOCK - 1)
    j = _iota(shape, 1)
    valid = (i - CMP_STRIDE * j + CT_OFF >= 0) & (j >= -n_old_blocks)
    bias = jnp.concatenate([ct_ref[r] for r in range(GROUP)], axis=0)
    step(n_old_blocks, pl.multiple_of(SUBLANES * b, SUBLANES), bias, valid)

    inv_l = 1.0 / jnp.maximum(l_ref[...], 1e-30)
    o = acc_ref[...] * inv_l
    pimp = imp_ref[...] * inv_l
    imp = pimp[0:Q_BLOCK]
    for r in range(1, GROUP):
        o_r = o[r * Q_BLOCK:(r + 1) * Q_BLOCK]
        imp = imp + pimp[r * Q_BLOCK:(r + 1) * Q_BLOCK]
    for r in range(GROUP):
        o_ref[:, r * HEAD_DIM:(r + 1) * HEAD_DIM] = o[r * Q_BLOCK:(r + 1) * Q_BLOCK]
    qpos = b * Q_BLOCK + _iota((Q_BLOCK, n_sb), 0)
    sel_ref[...] = _top_k_mask(_block_scores(imp, qpos))


def _prompt_cmp(rel_bias, pbf3, ckp, cvp, ct):
    n, t, _ = pbf3.shape
    nqb = t // Q_BLOCK
    n_sb = t // SEL_BLOCK
    rows = GROUP * Q_BLOCK
    qw = GROUP * HEAD_DIM
    ck_rows = ckp.shape[2]
    return pl.pallas_call(
        functools.partial(_pcmp_kernel, n_sb=n_sb),
        grid=(n, N_KV, nqb),
        in_specs=[pl.BlockSpec(memory_space=pltpu.SMEM),
                  pl.BlockSpec((None, Q_BLOCK, qw), lambda i, g, b: (i, b, g)),
                  pl.BlockSpec((None, None, ck_rows, HEAD_DIM), lambda i, g, b: (i, g, 0, 0)),
                  pl.BlockSpec((None, None, ck_rows, HEAD_DIM), lambda i, g, b: (i, g, 0, 0)),
                  pl.BlockSpec((GROUP, Q_BLOCK, Q_BLOCK), lambda i, g, b: (g, 0, 0))],
        out_specs=[pl.BlockSpec((None, Q_BLOCK, qw), lambda i, g, b: (i, b, g)),
                   pl.BlockSpec((None, None, Q_BLOCK, n_sb), lambda i, g, b: (i, g, b, 0))],
        out_shape=[jax.ShapeDtypeStruct((n, t, ATT_W), F32),
                   jax.ShapeDtypeStruct((n, N_KV, t, n_sb), F32)],
        scratch_shapes=[pltpu.VMEM((rows, 1), F32), pltpu.VMEM((rows, 1), F32),
                        pltpu.VMEM((rows, HEAD_DIM), F32), pltpu.VMEM((rows, n_sb), F32)],
        compiler_params=_cp("parallel", "parallel", "arbitrary"),
        name="prompt_cmp",
    )(rel_bias, pbf3, ckp, cvp, ct)


def _attn_init(m_ref, l_ref, acc_ref):
    m_ref[...] = jnp.full(m_ref.shape, NEG, F32)
    l_ref[...] = jnp.zeros(l_ref.shape, F32)
    acc_ref[...] = jnp.zeros(acc_ref.shape, F32)


def _attn_write(o_ref, l_ref, acc_ref):
    o = acc_ref[...] / jnp.maximum(l_ref[...], 1e-30)
    for r in range(GROUP):
        o_ref[:, r * HEAD_DIM:(r + 1) * HEAD_DIM] = o[r * Q_BLOCK:(r + 1) * Q_BLOCK]


def _tile4(x):
    return jnp.concatenate([x] * GROUP, axis=0)


def _psel_kernel(rb_ref, q_ref, k_ref, v_ref, sel_ref, bt_ref, o_ref, m_ref, l_ref, acc_ref):
    g, b = pl.program_id(1), pl.program_id(2)
    q = _stack_heads(q_ref)
    _attn_init(m_ref, l_ref, acc_ref)
    selm = sel_ref[...].astype(BF16)
    n_sb = selm.shape[1]
    tile = (Q_BLOCK, Q_BLOCK)
    row, lane = _iota(tile, 0), _iota(tile, 1)
    rb_last = _per_head_rows([rb_ref[NUM_BUCKETS - 1, g * GROUP + r] for r in range(GROUP)], Q_BLOCK)

    def process(t, bias, causal):
        start = pl.multiple_of(t * Q_BLOCK, Q_BLOCK)
        kt = k_ref[pl.ds(start, Q_BLOCK), :]
        vt = v_ref[pl.ds(start, Q_BLOCK), :]
        blocks_per_tile = Q_BLOCK // SEL_BLOCK
        expand = (_iota((n_sb, Q_BLOCK), 0) == blocks_per_tile * t + (_iota((n_sb, Q_BLOCK), 1) >> 6)).astype(BF16)
        picked = _dot(selm, expand) > 0.5
        if causal:
            picked = picked & (lane <= row)
        mask = _tile4(jnp.where(picked, 0.0, NEG))
        s = _dot_nt(q, kt) * ATT_SCALE + bias + mask
        _online_update(s, vt, m_ref, l_ref, acc_ref)

    process(b, jnp.concatenate([bt_ref[r, 0] for r in range(GROUP)], axis=0), True)

    @pl.when(b >= 1)
    def _():
        process(b - 1, jnp.concatenate([bt_ref[r, 1] for r in range(GROUP)], axis=0), False)

    def far(t, carry):
        process(t, rb_last, False)
        return carry

    lax.fori_loop(0, jnp.maximum(b - 1, 0), far, 0)
    _attn_write(o_ref, l_ref, acc_ref)


def _pwin_kernel(rb_ref, q_ref, k_ref, v_ref, bt_ref, o_ref, m_ref, l_ref, acc_ref):
    g, b = pl.program_id(1), pl.program_id(2)
    q = _stack_heads(q_ref)
    _attn_init(m_ref, l_ref, acc_ref)
    tile = (Q_BLOCK, Q_BLOCK)
    row, lane = _iota(tile, 0), _iota(tile, 1)
    rb_last = _per_head_rows([rb_ref[NUM_BUCKETS - 1, g * GROUP + r] for r in range(GROUP)], Q_BLOCK)

    def process(t, bias, keep):
        start = pl.multiple_of(t * Q_BLOCK, Q_BLOCK)
        kt = k_ref[pl.ds(start, Q_BLOCK), :]
        vt = v_ref[pl.ds(start, Q_BLOCK), :]
        s = _dot_nt(q, kt) * ATT_SCALE + bias
        if keep is not None:
            s = s + _tile4(jnp.where(keep, 0.0, NEG))
        _online_update(s, vt, m_ref, l_ref, acc_ref)

    n_back = WINDOW // Q_BLOCK
    process(b, jnp.concatenate([bt_ref[r, 0] for r in range(GROUP)], axis=0), lane <= row)
    for back in range(1, n_back + 1):
        if back == 1:
            bias = jnp.concatenate([bt_ref[r, 1] for r in range(GROUP)], axis=0)
        else:
            bias = rb_last
        keep = (lane > row) if back == n_back else None

        @pl.when(b >= back)
        def _(back=back, bias=bias, keep=keep):
            process(b - back, bias, keep)

    _attn_write(o_ref, l_ref, acc_ref)


def _prompt_sel(rel_bias, pbf3, selmask, bt):
    n, t, _ = pbf3.shape
    nqb = t // Q_BLOCK
    n_sb = selmask.shape[-1]
    rows = GROUP * Q_BLOCK
    qw = GROUP * HEAD_DIM
    return pl.pallas_call(
        _psel_kernel,
        grid=(n, N_KV, nqb),
        in_specs=[pl.BlockSpec(memory_space=pltpu.SMEM),
                  pl.BlockSpec((None, Q_BLOCK, qw), lambda i, g, b: (i, b, g)),
                  pl.BlockSpec((None, t, HEAD_DIM), lambda i, g, b: (i, 0, B_KS // HEAD_DIM + g)),
                  pl.BlockSpec((None, t, HEAD_DIM), lambda i, g, b: (i, 0, B_VS // HEAD_DIM + g)),
                  pl.BlockSpec((None, None, Q_BLOCK, n_sb), lambda i, g, b: (i, g, b, 0)),
                  pl.BlockSpec((GROUP, 2, Q_BLOCK, Q_BLOCK), lambda i, g, b: (g, 0, 0, 0))],
        out_specs=pl.BlockSpec((None, Q_BLOCK, qw), lambda i, g, b: (i, b, g)),
        out_shape=jax.ShapeDtypeStruct((n, t, ATT_W), F32),
        scratch_shapes=[pltpu.VMEM((rows, 1), F32), pltpu.VMEM((rows, 1), F32), pltpu.VMEM((rows, HEAD_DIM), F32)],
        compiler_params=_cp("parallel", "parallel", "arbitrary"),
        name="prompt_sel",
    )(rel_bias, pbf3, pbf3, pbf3, selmask, bt)


def _prompt_win(rel_bias, pbf3, bt):
    n, t, _ = pbf3.shape
    nqb = t // Q_BLOCK
    rows = GROUP * Q_BLOCK
    qw = GROUP * HEAD_DIM
    return pl.pallas_call(
        _pwin_kernel,
        grid=(n, N_KV, nqb),
        in_specs=[pl.BlockSpec(memory_space=pltpu.SMEM),
                  pl.BlockSpec((None, Q_BLOCK, qw), lambda i, g, b: (i, b, g)),
                  pl.BlockSpec((None, t, HEAD_DIM), lambda i, g, b: (i, 0, B_KW // HEAD_DIM + g)),
                  pl.BlockSpec((None, t, HEAD_DIM), lambda i, g, b: (i, 0, B_VW // HEAD_DIM + g)),
                  pl.BlockSpec((GROUP, 2, Q_BLOCK, Q_BLOCK), lambda i, g, b: (g, 0, 0, 0))],
        out_specs=pl.BlockSpec((None, Q_BLOCK, qw), lambda i, g, b: (i, b, g)),
        out_shape=jax.ShapeDtypeStruct((n, t, ATT_W), F32),
        scratch_shapes=[pltpu.VMEM((rows, 1), F32), pltpu.VMEM((rows, 1), F32), pltpu.VMEM((rows, HEAD_DIM), F32)],
        compiler_params=_cp("parallel", "parallel", "arbitrary"),
        name="prompt_win",
    )(rel_bias, pbf3, pbf3, pbf3, bt)


def _sample_rows(t_s):
    rows = GROUP * t_s
    ridx = _iota((rows, 1), 0)
    return rows, ridx % t_s, ridx // t_s


def _rb_col(rb_ref, g, r_of_row):
    def rb_of(k):
        col = jnp.zeros(r_of_row.shape, F32)
        for r in range(GROUP):
            col = jnp.where(r_of_row == r, rb_ref[k, g * GROUP + r], col)
        return col
    return rb_of


def _new_key_scores(qf, kn, q_idx, rb_of):
    t_s = kn.shape[0]
    cols = []
    for j in range(t_s):
        s = jnp.sum(qf * kn[j:j + 1, :], axis=1, keepdims=True) * ATT_SCALE
        cols.append(s + _rel_bias(q_idx - j, rb_of))
    return cols


def _scmp_kernel(rb_ref, q_ref, ck_ref, cv_ref, ov_ref, o_ref, sel_ref, *, past, t_s, nc):
    rows, q_idx, r_of_row = _sample_rows(t_s)
    qpos = past + q_idx
    n_blk = ck_ref.shape[1]
    for g in range(N_KV):
        rb_of = _rb_col(rb_ref, g, r_of_row)
        q = q_ref[g]
        c = _iota((rows, n_blk), 1)
        dist = qpos - (c * CMP_STRIDE + (L_CMP - 1))
        valid = (dist >= 0) & (c < nc)
        s = _dot_nt(q, ck_ref[g].astype(BF16)) * ATT_SCALE + _rel_bias(dist, rb_of)
        s = jnp.where(valid, s, NEG)
        m = jnp.max(s, axis=1, keepdims=True)
        p = jnp.where(valid, jnp.exp(s - m), 0.0)
        p = p / jnp.maximum(jnp.sum(p, axis=1, keepdims=True), 1e-30)
        o_ref[g] = _dot(p.astype(BF16), cv_ref[g].astype(BF16))
        imp = _dot_split(p, ov_ref[...])
        tot = imp
        for r in range(1, GROUP):
            tot = tot + pltpu.roll(imp, r * t_s, 0)
        sel_ref[g] = _top_k_mask(_block_scores(tot, qpos))


def _sample_cmp(rel_bias, qs, ck, cv, ov, past, t_s, nc):
    ns = qs.shape[0]
    rows = GROUP * t_s
    n_blk = ck.shape[2]
    n_sbp = ov.shape[1]
    return pl.pallas_call(
        functools.partial(_scmp_kernel, past=past, t_s=t_s, nc=nc),
        grid=(ns,),
        in_specs=[pl.BlockSpec(memory_space=pltpu.SMEM),
                  pl.BlockSpec((None, N_KV, rows, HEAD_DIM), lambda i: (i, 0, 0, 0)),
                  pl.BlockSpec((None, N_KV, n_blk, HEAD_DIM), lambda i: (i, 0, 0, 0)),
                  pl.BlockSpec((None, N_KV, n_blk, HEAD_DIM), lambda i: (i, 0, 0, 0)),
                  pl.BlockSpec(ov.shape, lambda i: (0, 0))],
        out_specs=[pl.BlockSpec((None, N_KV, rows, HEAD_DIM), lambda i: (i, 0, 0, 0)),
                   pl.BlockSpec((None, N_KV, rows, n_sbp), lambda i: (i, 0, 0, 0))],
        out_shape=[jax.ShapeDtypeStruct((ns, N_KV, rows, HEAD_DIM), F32),
                   jax.ShapeDtypeStruct((ns, N_KV, rows, n_sbp), F32)],
        compiler_params=_cp("parallel"),
        name="sample_cmp",
    )(rel_bias, qs, ck, cv, ov)


def _ssel_kernel(pt_ref, rb_ref, q_ref, sel_ref, kn_ref, vn_ref, *refs, past, t_s):
    k_pages = refs[:PAGES_PER_STEP]
    v_pages = refs[PAGES_PER_STEP:2 * PAGES_PER_STEP]
    o_ref, m_ref, l_ref, acc_ref, s_ref = refs[2 * PAGES_PER_STEP:]
    step = pl.program_id(1)
    last = pl.num_programs(1) - 1
    rows, q_idx, r_of_row = _sample_rows(t_s)
    qpos = past + q_idx
    keys = PAGES_PER_STEP * PAGE
    n_sbp = sel_ref.shape[2]

    @pl.when(step == 0)
    def _():
        for g in range(N_KV):
            rb_of = _rb_col(rb_ref, g, r_of_row)
            qf = q_ref[g].astype(F32)
            kn, vn = kn_ref[g], vn_ref[g]
            selm = sel_ref[g]
            cols = _new_key_scores(qf, kn, q_idx, rb_of)
            valid = []
            for j in range(t_s):
                blk = (past + j) // SEL_BLOCK
                valid.append((selm[:, blk:blk + 1] > 0.5) & (j <= q_idx))
            m = jnp.full((rows, 1), NEG, F32)
            for j in range(t_s):
                m = jnp.maximum(m, jnp.where(valid[j], cols[j], NEG))
            l = jnp.zeros((rows, 1), F32)
            acc = jnp.zeros((rows, HEAD_DIM), F32)
            for j in range(t_s):
                e = jnp.where(valid[j], jnp.exp(cols[j] - m), 0.0)
                l = l + e
                acc = acc + e * vn[j:j + 1, :]
            m_ref[g], l_ref[g], acc_ref[g] = m, l, acc

    @pl.when(step >= 1)
    def _():
        grp = step - 1
        tok = grp * keys + _iota((rows, keys), 1)
        dist = qpos - tok
        blocks_per_step = keys // SEL_BLOCK
        expand = (_iota((n_sbp, keys), 0) == blocks_per_step * grp + (_iota((n_sbp, keys), 1) >> 6)).astype(BF16)
        for g in range(N_KV):
            rb_of = _rb_col(rb_ref, g, r_of_row)
            cols = slice(g * HEAD_DIM, (g + 1) * HEAD_DIM)
            kt = jnp.concatenate([p[:, cols] for p in k_pages], axis=0).astype(BF16)
            vt = jnp.concatenate([p[:, cols] for p in v_pages], axis=0).astype(BF16)
            raw = _dot_nt(q_ref[g], kt) * ATT_SCALE
            s_ref[...] = raw + rb_of(NUM_BUCKETS - 1)

            @pl.when(step == last)
            def _():
                s_ref[...] = raw + _rel_bias(dist, rb_of)

            picked = (_dot(sel_ref[g].astype(BF16), expand) > 0.5) & (dist >= 0)
            s = s_ref[...] + jnp.where(picked, 0.0, NEG)
            _online_update(s, vt, m_ref.at[g], l_ref.at[g], acc_ref.at[g])

    @pl.when(step == last)
    def _():
        for g in range(N_KV):
            o_ref[g] = acc_ref[g] / jnp.maximum(l_ref[g], 1e-30)


def _sample_sel(page_table, rel_bias, qs, selmask, ksn, vsn, cache_k, cache_v, past, t_s):
    ns, n_pages = page_table.shape
    rows = GROUP * t_s
    n_sbp = selmask.shape[-1]
    n_grp = n_pages // PAGES_PER_STEP

    def page_spec(k):
        return pl.BlockSpec((None, PAGE, KV_W),
                            lambda i, s, pt: (pt[i, jnp.maximum(s - 1, 0) * PAGES_PER_STEP + k], 0, 0))

    whole = lambda i, s, pt: (i, 0, 0, 0)
    grid_spec = pltpu.PrefetchScalarGridSpec(
        num_scalar_prefetch=1,
        grid=(ns, n_grp + 1),
        in_specs=[pl.BlockSpec(memory_space=pltpu.SMEM),
                  pl.BlockSpec((None, N_KV, rows, HEAD_DIM), whole),
                  pl.BlockSpec((None, N_KV, rows, n_sbp), whole),
                  pl.BlockSpec((None, N_KV, t_s, HEAD_DIM), whole),
                  pl.BlockSpec((None, N_KV, t_s, HEAD_DIM), whole)]
                 + [page_spec(k) for k in range(PAGES_PER_STEP)] * 2,
        out_specs=pl.BlockSpec((None, N_KV, rows, HEAD_DIM), whole),
        scratch_shapes=[pltpu.VMEM((N_KV, rows, 1), F32), pltpu.VMEM((N_KV, rows, 1), F32),
                        pltpu.VMEM((N_KV, rows, HEAD_DIM), F32),
                        pltpu.VMEM((rows, PAGES_PER_STEP * PAGE), F32)],
    )
    return pl.pallas_call(
        functools.partial(_ssel_kernel, past=past, t_s=t_s),
        grid_spec=grid_spec,
        out_shape=jax.ShapeDtypeStruct((ns, N_KV, rows, HEAD_DIM), F32),
        compiler_params=_cp("parallel", "arbitrary"),
        name="sample_sel",
    )(page_table, rel_bias, qs, selmask, ksn, vsn, *([cache_k] * PAGES_PER_STEP), *([cache_v] * PAGES_PER_STEP))


def _swin_kernel(rb_ref, q_ref, kn_ref, vn_ref, ck_ref, cv_ref, o_ref, *, past, t_s):
    rows, q_idx, r_of_row = _sample_rows(t_s)
    wk = ck_ref.shape[0]
    kpos = past - wk + _iota((rows, wk), 1)
    dist = past + q_idx - kpos
    valid = (dist >= 0) & (dist < WINDOW) & (kpos >= 0)
    for g in range(N_KV):
        rb_of = _rb_col(rb_ref, g, r_of_row)
        cols = slice(g * HEAD_DIM, (g + 1) * HEAD_DIM)
        kt = ck_ref[:, cols].astype(BF16)
        vt = cv_ref[:, cols].astype(BF16)
        s = _dot_nt(q_ref[g], kt) * ATT_SCALE + _rel_bias(dist, rb_of)
        s = jnp.where(valid, s, NEG)
        kn, vn = kn_ref[g], vn_ref[g]
        new = _new_key_scores(q_ref[g].astype(F32), kn, q_idx, rb_of)
        new_ok = [j <= q_idx for j in range(t_s)]
        m = jnp.max(s, axis=1, keepdims=True)
        for j in range(t_s):
            m = jnp.maximum(m, jnp.where(new_ok[j], new[j], NEG))
        e = jnp.where(valid, jnp.exp(s - m), 0.0)
        l = jnp.sum(e, axis=1, keepdims=True)
        acc = _dot(e.astype(BF16), vt)
        for j in range(t_s):
            ej = jnp.where(new_ok[j], jnp.exp(new[j] - m), 0.0)
            l = l + ej
            acc = acc + ej * vn[j:j + 1, :]
        o_ref[g] = acc / jnp.maximum(l, 1e-30)


def _sample_win(rel_bias, qs, kwn, vwn, cwk, cwv, past, t_s):
    ns = qs.shape[0]
    rows = GROUP * t_s
    wk = cwk.shape[1]
    whole = lambda i: (i, 0, 0, 0)
    return pl.pallas_call(
        functools.partial(_swin_kernel, past=past, t_s=t_s),
        grid=(ns,),
        in_specs=[pl.BlockSpec(memory_space=pltpu.SMEM),
                  pl.BlockSpec((None, N_KV, rows, HEAD_DIM), whole),
                  pl.BlockSpec((None, N_KV, t_s, HEAD_DIM), whole),
                  pl.BlockSpec((None, N_KV, t_s, HEAD_DIM), whole),
                  pl.BlockSpec((None, wk, KV_W), lambda i: (i, 0, 0)),
                  pl.BlockSpec((None, wk, KV_W), lambda i: (i, 0, 0))],
        out_specs=pl.BlockSpec((None, N_KV, rows, HEAD_DIM), whole),
        out_shape=jax.ShapeDtypeStruct((ns, N_KV, rows, HEAD_DIM), F32),
        compiler_params=_cp("parallel"),
        name="sample_win",
    )(rel_bias, qs, kwn, vwn, cwk, cwv)


EXT_TOP = 8


def _dn_kernel(xq_ref, xk_ref, xv_ref, z_ref, sm_ref, cprev_ref, s0_ref, cw_ref, alog_ref, dtb_ref, nw_ref,
               o_ref, sout_ref, ext_ref, st_ref, *, tc):
    c = pl.program_id(1)
    ch = DN_CHUNK
    keep = DN_CONV - 1

    @pl.when(c == 0)
    def _():
        ext_ref[EXT_TOP - keep:EXT_TOP, :] = cprev_ref[...]
        st_ref[...] = s0_ref[...]

    @pl.when(c > 0)
    def _():
        ext_ref[EXT_TOP - keep:EXT_TOP, :] = ext_ref[EXT_TOP + ch - keep:EXT_TOP + ch, :]

    ext_ref[EXT_TOP:EXT_TOP + tc, 0:DN_W] = xq_ref[...]
    ext_ref[EXT_TOP:EXT_TOP + tc, DN_W:2 * DN_W] = xk_ref[...]
    ext_ref[EXT_TOP:EXT_TOP + tc, 2 * DN_W:3 * DN_W] = xv_ref[...]
    if tc < ch:
        ext_ref[EXT_TOP + tc:EXT_TOP + ch, :] = jnp.zeros((ch - tc, 3 * DN_W), F32)

    y = ext_ref[EXT_TOP - keep:EXT_TOP - keep + ch, :] * cw_ref[0:1, :]
    for j in range(1, DN_CONV):
        y = y + ext_ref[EXT_TOP - keep + j:EXT_TOP - keep + j + ch, :] * cw_ref[j:j + 1, :]
    y = _silu(y)

    sq = (ch, ch)
    ri, ci = _iota(sq, 0), _iota(sq, 1)
    tril = ci <= ri
    strict = ci < ri
    tril_f = tril.astype(F32)
    triu_f = (ri <= ci).astype(F32)
    ones_f = jnp.ones(sq, F32)
    eye_f = (ri == ci).astype(F32)

    sm = sm_ref[...]
    a_all = sm[:, SM_A:SM_A + N_DN]
    b_all = sm[:, SM_B:SM_B + N_DN]
    xs = a_all + dtb_ref[...]
    softplus = jnp.maximum(xs, 0.0) + jnp.log(1.0 + jnp.exp(-jnp.abs(xs)))
    g_all = -jnp.exp(alog_ref[...]) * softplus
    beta_all = _sigmoid(b_all)
    if tc < ch:
        live = _iota((ch, 1), 0) < tc
        pad = jnp.zeros((ch - tc, N_DN), F32)
        g_all = jnp.concatenate([g_all, pad], axis=0)
        beta_all = jnp.concatenate([beta_all, pad], axis=0)
        y = jnp.where(live, y, 0.0)

    for h in range(N_DN):
        hs = slice(h * HEAD_DIM, (h + 1) * HEAD_DIM)
        q = y[:, h * HEAD_DIM:(h + 1) * HEAD_DIM]
        k = y[:, DN_W + h * HEAD_DIM:DN_W + (h + 1) * HEAD_DIM]
        v = y[:, 2 * DN_W + h * HEAD_DIM:2 * DN_W + (h + 1) * HEAD_DIM]
        q = q * lax.rsqrt(jnp.sum(q * q, axis=-1, keepdims=True) + RMS_EPS) * (HEAD_DIM ** -0.5)
        k = k * lax.rsqrt(jnp.sum(k * k, axis=-1, keepdims=True) + RMS_EPS)
        gb = jnp.broadcast_to(g_all[:, h:h + 1], sq)
        beta = beta_all[:, h:h + 1]
        gc_rows = _dot(tril_f, gb, HI)
        gc_cols = _dot(ones_f, gb * triu_f, HI)
        decay = jnp.exp(jnp.where(tril, gc_rows - gc_cols, NEG))
        gc = gc_rows[:, 0:1]
        gc_last = gc_rows[ch - 1:ch, 0:1]
        egc = jnp.exp(gc)
        kb = k * beta
        a_mat = jnp.where(strict, _dot_nt(kb, k, HI) * decay, 0.0)
        pw = -a_mat
        inv = eye_f + pw
        for _ in range(int(math.log2(ch)) - 1):
            pw = _dot(pw, pw, HI)
            inv = inv + _dot(inv, pw, HI)
        sol = _dot(inv, jnp.concatenate([v * beta, kb * egc], axis=1), HI)
        u, w = sol[:, :HEAD_DIM], sol[:, HEAD_DIM:]
        attn = jnp.where(tril, _dot_nt(q, k, HI) * decay, 0.0)
        s_h = st_ref[h]
        v_new = u - _dot(w, s_h, HI)
        o = _dot(q * egc, s_h, HI) + _dot(attn, v_new, HI)
        st_ref[h] = s_h * jnp.exp(gc_last) + _dot_tn(k * jnp.exp(gc_last - gc), v_new, HI)
        o = o[0:tc]
        o = o * lax.rsqrt(jnp.mean(o * o, axis=-1, keepdims=True) + RMS_EPS) * nw_ref[...]
        o_ref[:, hs] = (o * _silu(z_ref[:, hs])).astype(o_ref.dtype)

    @pl.when(c == pl.num_programs(1) - 1)
    def _():
        sout_ref[...] = st_ref[...]


def _deltanet(proj3, conv_prev, s0, conv_w, a_log, dt_bias, norm_w):
    n, t, _ = proj3.shape
    tc = min(DN_CHUNK, t)
    assert t % tc == 0 and t >= DN_CONV - 1
    nck = t // tc
    col = lambda cb: (lambda i, c: (i, c, cb))
    const2 = lambda i, c: (0, 0)
    return pl.pallas_call(
        functools.partial(_dn_kernel, tc=tc),
        grid=(n, nck),
        in_specs=[pl.BlockSpec((None, tc, DN_W), col(C_DQ // DN_W)),
                  pl.BlockSpec((None, tc, DN_W), col(C_DK // DN_W)),
                  pl.BlockSpec((None, tc, DN_W), col(C_DV // DN_W)),
                  pl.BlockSpec((None, tc, DN_W), col(C_Z // DN_W)),
                  pl.BlockSpec((None, tc, LANES), col(C_SM // LANES)),
                  pl.BlockSpec((None, DN_CONV - 1, 3 * DN_W), lambda i, c: (i, 0, 0)),
                  pl.BlockSpec((None, N_DN, HEAD_DIM, HEAD_DIM), lambda i, c: (i, 0, 0, 0)),
                  pl.BlockSpec(conv_w.shape, const2),
                  pl.BlockSpec(a_log.shape, const2),
                  pl.BlockSpec(dt_bias.shape, const2),
                  pl.BlockSpec(norm_w.shape, const2)],
        out_specs=[pl.BlockSpec((None, tc, DN_W), lambda i, c: (i, c, 0)),
                   pl.BlockSpec((None, N_DN, HEAD_DIM, HEAD_DIM), lambda i, c: (i, 0, 0, 0))],
        out_shape=[jax.ShapeDtypeStruct((n, t, DN_W), BF16),
                   jax.ShapeDtypeStruct((n, N_DN, HEAD_DIM, HEAD_DIM), F32)],
        scratch_shapes=[pltpu.VMEM((EXT_TOP + DN_CHUNK, 3 * DN_W), F32),
                        pltpu.VMEM((N_DN, HEAD_DIM, HEAD_DIM), F32)],
        compiler_params=_cp("parallel", "arbitrary"),
        name="deltanet",
    )(proj3, proj3, proj3, proj3, proj3, conv_prev, s0, conv_w, a_log, dt_bias, norm_w)


def _outproj_kernel(x_ref, oc_ref, os_ref, ow_ref, od_ref, sm_ref, w_ref, npost_ref, npre_ref,
                    x1_ref, hn_ref, mix_ref):
    gates = _sigmoid(sm_ref[:, SM_GATE:SM_GATE + 3 * N_HEADS])
    for h in range(N_HEADS):
        hs = slice(h * HEAD_DIM, (h + 1) * HEAD_DIM)
        o = (gates[:, 3 * h:3 * h + 1] * oc_ref[:, hs] + gates[:, 3 * h + 1:3 * h + 2] * os_ref[:, hs]
             + gates[:, 3 * h + 2:3 * h + 3] * ow_ref[:, hs])
        mix_ref[:, hs] = o.astype(BF16)
    mix_ref[:, ATT_W:] = od_ref[...]
    mix = _dot(mix_ref[...], w_ref[...])
    x1 = x_ref[...] + _rms(mix, npost_ref[...])
    x1_ref[...] = x1
    hn_ref[...] = _rms(x1, npre_ref[...]).astype(BF16)


def _out_proj(x2d, o_cmp, o_sel, o_win, o_dn, proj2d, w_out_bf, n_post, n_pre):
    r, d = x2d.shape
    tm = min(256, r)
    row = lambda i: (i, 0)
    const = lambda i: (0, 0)
    return pl.pallas_call(
        _outproj_kernel,
        grid=(r // tm,),
        in_specs=[pl.BlockSpec((tm, d), row),
                  pl.BlockSpec((tm, ATT_W), row), pl.BlockSpec((tm, ATT_W), row), pl.BlockSpec((tm, ATT_W), row),
                  pl.BlockSpec((tm, DN_W), row),
                  pl.BlockSpec((tm, LANES), lambda i: (i, C_SM // LANES)),
                  pl.BlockSpec(w_out_bf.shape, const),
                  pl.BlockSpec((1, d), const), pl.BlockSpec((1, d), const)],
        out_specs=[pl.BlockSpec((tm, d), row), pl.BlockSpec((tm, d), row)],
        out_shape=[jax.ShapeDtypeStruct((r, d), F32), jax.ShapeDtypeStruct((r, d), BF16)],
        scratch_shapes=[pltpu.VMEM((tm, ATT_W + DN_W), BF16)],
        compiler_params=_cp("parallel"),
        name="out_proj",
    )(x2d, o_cmp, o_sel, o_win, o_dn, proj2d, w_out_bf, n_post, n_pre)


def _ffn_conv_gate(extg_ref, extv_ref, cwg_ref, cwv_ref, tm, fix=None):
    keep = FFN_CONV - 1
    outs = []
    for ext_ref, cw_ref, idx in ((extg_ref, cwg_ref, 0), (extv_ref, cwv_ref, 1)):
        y = ext_ref[EXT_TOP:EXT_TOP + tm, :] * cw_ref[keep:keep + 1, :]
        for back in range(1, keep + 1):
            shifted = ext_ref[EXT_TOP - back:EXT_TOP - back + tm, :]
            if fix is not None:
                shifted = fix(shifted, back, idx)
            y = y + shifted * cw_ref[keep - back:keep - back + 1, :]
        outs.append(y)
    return _silu(outs[0]) * outs[1]


def _ffn_up_seq_kernel(hn_ref, wg_ref, wv_ref, cwg_ref, cwv_ref, pg_ref, pv_ref, act_ref, lg_ref, lv_ref,
                       extg_ref, extv_ref, *, tm, tiles_per_seq):
    i = pl.program_id(1)
    keep = FFN_CONV - 1
    first = i % tiles_per_seq == 0
    for ext_ref, p_ref in ((extg_ref, pg_ref), (extv_ref, pv_ref)):
        @pl.when(first)
        def _(ext_ref=ext_ref, p_ref=p_ref):
            ext_ref[EXT_TOP - keep:EXT_TOP, :] = p_ref[...]

        @pl.when(jnp.logical_not(first))
        def _(ext_ref=ext_ref):
            ext_ref[EXT_TOP - keep:EXT_TOP, :] = ext_ref[EXT_TOP + tm - keep:EXT_TOP + tm, :]

    hn = hn_ref[...]
    extg_ref[EXT_TOP:EXT_TOP + tm, :] = _dot(hn, wg_ref[...])
    extv_ref[EXT_TOP:EXT_TOP + tm, :] = _dot(hn, wv_ref[...])
    act_ref[...] = _ffn_conv_gate(extg_ref, extv_ref, cwg_ref, cwv_ref, tm).astype(BF16)

    @pl.when(i % tiles_per_seq == tiles_per_seq - 1)
    def _():
        lg_ref[...] = extg_ref[EXT_TOP + tm - keep:EXT_TOP + tm, :]
        lv_ref[...] = extv_ref[EXT_TOP + tm - keep:EXT_TOP + tm, :]


def _ffn_up_seq(hn, w_up_bf, conv_w, prev, n_seq, t):
    r, d = hn.shape
    dff = w_up_bf.shape[1] // 2
    tm = min(512, t)
    tn = 512
    nj = dff // tn
    tiles_per_seq = t // tm
    keep = FFN_CONV - 1
    return pl.pallas_call(
        functools.partial(_ffn_up_seq_kernel, tm=tm, tiles_per_seq=tiles_per_seq),
        grid=(nj, r // tm),
        in_specs=[pl.BlockSpec((tm, d), lambda j, i: (i, 0)),
                  pl.BlockSpec((d, tn), lambda j, i: (0, j)),
                  pl.BlockSpec((d, tn), lambda j, i: (0, nj + j)),
                  pl.BlockSpec((FFN_CONV, tn), lambda j, i: (0, j)),
                  pl.BlockSpec((FFN_CONV, tn), lambda j, i: (0, nj + j)),
                  pl.BlockSpec((None, keep, tn), lambda j, i: (i // tiles_per_seq, 0, j)),
                  pl.BlockSpec((None, keep, tn), lambda j, i: (i // tiles_per_seq, 0, nj + j))],
        out_specs=[pl.BlockSpec((tm, tn), lambda j, i: (i, j)),
                   pl.BlockSpec((None, keep, tn), lambda j, i: (i // tiles_per_seq, 0, j)),
                   pl.BlockSpec((None, keep, tn), lambda j, i: (i // tiles_per_seq, 0, j))],
        out_shape=[jax.ShapeDtypeStruct((r, dff), BF16),
                   jax.ShapeDtypeStruct((n_seq, keep, dff), F32),
                   jax.ShapeDtypeStruct((n_seq, keep, dff), F32)],
        scratch_shapes=[pltpu.VMEM((EXT_TOP + tm, tn), F32), pltpu.VMEM((EXT_TOP + tm, tn), F32)],
        compiler_params=_cp("parallel", "arbitrary"),
        name="ffn_up_seq",
    )(hn, w_up_bf, w_up_bf, conv_w, conv_w, prev, prev)


def _ffn_up_rows_kernel(hn_ref, wg_ref, wv_ref, cwg_ref, cwv_ref, bg1_ref, bg2_ref, bv1_ref, bv2_ref,
                        act_ref, ug_ref, uv_ref, extg_ref, extv_ref, *, tm, t):
    hn = hn_ref[...]
    ug = _dot(hn, wg_ref[...])
    uv = _dot(hn, wv_ref[...])
    ug_ref[...] = ug
    uv_ref[...] = uv
    extg_ref[0:EXT_TOP, :] = jnp.zeros((EXT_TOP, ug.shape[1]), F32)
    extv_ref[0:EXT_TOP, :] = jnp.zeros((EXT_TOP, uv.shape[1]), F32)
    extg_ref[EXT_TOP:EXT_TOP + tm, :] = ug
    extv_ref[EXT_TOP:EXT_TOP + tm, :] = uv
    pos = _iota((tm, 1), 0) % t
    bnd = ((bg1_ref, bg2_ref), (bv1_ref, bv2_ref))

    def fix(shifted, back, idx):
        return jnp.where(pos < back, bnd[idx][back - 1][...], shifted)

    act_ref[...] = _ffn_conv_gate(extg_ref, extv_ref, cwg_ref, cwv_ref, tm, fix).astype(BF16)


def _ffn_up_rows(hn, w_up_bf, conv_w, bnd1, bnd2, t):
    r, d = hn.shape
    dff = w_up_bf.shape[1] // 2
    tn = 512
    nj = dff // tn
    lo = lambda j: (0, j)
    hi = lambda j: (0, nj + j)
    return pl.pallas_call(
        functools.partial(_ffn_up_rows_kernel, tm=r, t=t),
        grid=(nj,),
        in_specs=[pl.BlockSpec((r, d), lambda j: (0, 0)),
                  pl.BlockSpec((d, tn), lo), pl.BlockSpec((d, tn), hi),
                  pl.BlockSpec((FFN_CONV, tn), lo), pl.BlockSpec((FFN_CONV, tn), hi),
                  pl.BlockSpec((r, tn), lo), pl.BlockSpec((r, tn), lo),
                  pl.BlockSpec((r, tn), hi), pl.BlockSpec((r, tn), hi)],
        out_specs=[pl.BlockSpec((r, tn), lo), pl.BlockSpec((r, tn), lo), pl.BlockSpec((r, tn), lo)],
        out_shape=[jax.ShapeDtypeStruct((r, dff), BF16),
                   jax.ShapeDtypeStruct((r, dff), F32), jax.ShapeDtypeStruct((r, dff), F32)],
        scratch_shapes=[pltpu.VMEM((EXT_TOP + r, tn), F32), pltpu.VMEM((EXT_TOP + r, tn), F32)],
        compiler_params=_cp("parallel"),
        name="ffn_up_rows",
    )(hn, w_up_bf, w_up_bf, conv_w, conv_w, bnd1, bnd2, bnd1, bnd2)


def _ffn_down_kernel(a_ref, w_ref, x_ref, nw_ref, y_ref, acc_ref):
    k = pl.program_id(1)

    @pl.when(k == 0)
    def _():
        acc_ref[...] = jnp.zeros(acc_ref.shape, F32)

    acc_ref[...] += _dot(a_ref[...], w_ref[...])

    @pl.when(k == pl.num_programs(1) - 1)
    def _():
        y_ref[...] = x_ref[...] + _rms(acc_ref[...], nw_ref[...])


def _ffn_down(act, w_down_bf, x1, nw):
    r, dff = act.shape
    d = x1.shape[1]
    tm = min(512, r)
    tk = 512
    return pl.pallas_call(
        _ffn_down_kernel,
        grid=(r // tm, dff // tk),
        in_specs=[pl.BlockSpec((tm, tk), lambda i, k: (i, k)),
                  pl.BlockSpec((tk, d), lambda i, k: (k, 0)),
                  pl.BlockSpec((tm, d), lambda i, k: (i, 0)),
                  pl.BlockSpec((1, d), lambda i, k: (0, 0))],
        out_specs=pl.BlockSpec((tm, d), lambda i, k: (i, 0)),
        out_shape=jax.ShapeDtypeStruct((r, d), F32),
        scratch_shapes=[pltpu.VMEM((tm, d), F32)],
        compiler_params=_cp("parallel", "arbitrary"),
        name="ffn_down",
    )(act, w_down_bf, x1, nw)


def _permute_w_in(w_in):
    o_q, o_kv = 0, ATT_W
    o_gate = o_kv + 6 * KV_W
    o_qkv = o_gate + 3 * N_HEADS
    o_z = o_qkv + 3 * DN_W
    o_a = o_z + DN_W
    o_b = o_a + N_DN
    perm = np.concatenate([np.arange(o_qkv, o_z), np.arange(o_z, o_a), np.arange(o_q, o_gate),
                           np.arange(o_gate, o_qkv), np.arange(o_a, o_b + N_DN)])
    w = jnp.take(w_in, jnp.asarray(perm, jnp.int32), axis=1)
    return jnp.pad(w, ((0, 0), (0, NP_COLS - perm.size))).astype(BF16)


def _cmp_weights(w1, b1, w2, pe):
    d = HEAD_DIM
    ratio = L_CMP // CMP_STRIDE
    w1f = w1.reshape(ratio, CMP_STRIDE * d, w1.shape[-1])
    w_cat = jnp.concatenate([w1f[r] for r in range(ratio)], axis=1).astype(BF16)
    return w_cat, pe.reshape(ratio, CMP_STRIDE * d), w1f, b1.reshape(1, -1), w2.astype(BF16)


def _row2(v):
    return v.reshape(1, -1)


def _heads_to_rows(o, ns, t_s):
    o = o.reshape(ns, N_KV, GROUP, t_s, HEAD_DIM)
    return jnp.transpose(o, (0, 3, 1, 2, 4)).reshape(ns * t_s, ATT_W)


def kernel(x_prompt, x_sample, cache_cmp_k, cache_cmp_v, cache_sel_k, cache_sel_v, cache_win_k, cache_win_v,
           state_dn, state_dn_conv, state_ffn_conv, page_table, rel_bias, w_in, w_out,
           norm_pre_mix, norm_post_mix, norm_pre_ffn, norm_post_ffn,
           cmp_w1_k, cmp_b1_k, cmp_w2_k, cmp_pe_k, cmp_w1_v, cmp_b1_v, cmp_w2_v, cmp_pe_v,
           dn_conv_w, dn_a_log, dn_dt_bias, dn_norm_w, ffn_w_up, ffn_conv_w, ffn_w_down):
    assert w_in.shape[0] == 1, "single layer"
    nb_p, t_p, d = x_prompt.shape
    nb_s, t_s, _ = x_sample.shape
    n_pages = page_table.shape[1]
    past = n_pages * PAGE
    dff = ffn_w_down.shape[1]
    assert t_p % 2048 == 0 or t_p in (256, 512, 1024)
    assert n_pages % PAGES_PER_STEP == 0

    w_in_bf = _permute_w_in(w_in[0])
    w_out_bf = w_out[0].astype(BF16)
    w_up_bf = ffn_w_up[0].astype(BF16)
    w_down_bf = ffn_w_down[0].astype(BF16)
    cw_k = _cmp_weights(cmp_w1_k[0], cmp_b1_k[0], cmp_w2_k[0], cmp_pe_k[0])
    cw_v = _cmp_weights(cmp_w1_v[0], cmp_b1_v[0], cmp_w2_v[0], cmp_pe_v[0])
    bt, ct = _bias_tiles(rel_bias)
    dn_args = (dn_conv_w[0], _row2(dn_a_log[0]), _row2(dn_dt_bias[0]), _row2(dn_norm_w[0]))

    def dense_tail(x2d, o_cmp, o_sel, o_win, o_dn, proj):
        return _out_proj(x2d, o_cmp, o_sel, o_win, o_dn, proj, w_out_bf,
                         _row2(norm_post_mix[0]), _row2(norm_pre_ffn[0]))

    xp2 = x_prompt.reshape(nb_p * t_p, d)
    proj_p, pbf_p = _in_proj(xp2, _row2(norm_pre_mix[0]), w_in_bf)
    proj3 = proj_p.reshape(nb_p, t_p, NP_COLS)
    pbf3 = pbf_p.reshape(nb_p, t_p, BF_W)
    ck_p = _cmp_mlp(_cmp_seg_prompt(proj3, C_KC, cw_k[0]), *cw_k[1:], CK_FRONT, CK_BACK)
    cv_p = _cmp_mlp(_cmp_seg_prompt(proj3, C_VC, cw_v[0]), *cw_v[1:], CK_FRONT, CK_BACK)
    o_cmp_p, selmask_p = _prompt_cmp(rel_bias, pbf3, ck_p, cv_p, ct)
    o_sel_p = _prompt_sel(rel_bias, pbf3, selmask_p, bt)
    o_win_p = _prompt_win(rel_bias, pbf3, bt)
    o_dn_p, dn_state_p = _deltanet(
        proj3, jnp.zeros((nb_p, DN_CONV - 1, 3 * DN_W), F32),
        jnp.zeros((nb_p, N_DN, HEAD_DIM, HEAD_DIM), F32), *dn_args)
    x1_p, hn_p = dense_tail(xp2, o_cmp_p.reshape(-1, ATT_W), o_sel_p.reshape(-1, ATT_W),
                            o_win_p.reshape(-1, ATT_W), o_dn_p.reshape(-1, DN_W), proj_p)
    act_p, ffc_g, ffc_v = _ffn_up_seq(hn_p, w_up_bf, ffn_conv_w[0],
                                      jnp.zeros((nb_p, FFN_CONV - 1, 2 * dff), F32), nb_p, t_p)
    y_p = _ffn_down(act_p, w_down_bf, x1_p, _row2(norm_post_ffn[0])).reshape(nb_p, t_p, d)

    kv_rows = lambda p3, c: p3[:, :, c:c + KV_W].reshape(1, p3.shape[0], p3.shape[1], N_KV, HEAD_DIM)
    keep_p = min(WINDOW, t_p)
    prompt_state = (
        kv_rows(proj3, C_KC), kv_rows(proj3, C_VC), kv_rows(proj3, C_KS), kv_rows(proj3, C_VS),
        kv_rows(proj3, C_KW)[:, :, t_p - keep_p:], kv_rows(proj3, C_VW)[:, :, t_p - keep_p:],
        dn_state_p[None],
        proj3[:, t_p - (DN_CONV - 1):, C_DQ:C_DQ + 3 * DN_W][None],
        jnp.concatenate([ffc_g, ffc_v], axis=-1)[None],
    )

    xs2 = x_sample.reshape(nb_s * t_s, d)
    proj_s, pbf_s = _in_proj(xs2, _row2(norm_pre_mix[0]), w_in_bf)
    proj3s = proj_s.reshape(nb_s, t_s, NP_COLS)
    rows = GROUP * t_s
    qs = jnp.transpose(pbf_s[:, B_Q:B_Q + ATT_W].reshape(nb_s, t_s, N_KV, GROUP, HEAD_DIM),
                       (0, 2, 3, 1, 4)).reshape(nb_s, N_KV, rows, HEAD_DIM)
    new_rows = lambda c: jnp.transpose(proj3s[:, :, c:c + KV_W].reshape(nb_s, t_s, N_KV, HEAD_DIM), (0, 2, 1, 3))
    pages = lambda cache: cache[0].reshape(cache.shape[1], PAGE, KV_W)

    ck_s = _cmp_mlp(_cmp_seg_pages(pages(cache_cmp_k), page_table, cw_k[0]), *cw_k[1:], 0, 0)
    cv_s = _cmp_mlp(_cmp_seg_pages(pages(cache_cmp_v), page_table, cw_v[0]), *cw_v[1:], 0, 0)
    seq_len = past + t_s
    nc_s = (seq_len - L_CMP) // CMP_STRIDE + 1
    n_blk = ck_s.shape[2]
    n_sb = -(-seq_len // SEL_BLOCK)
    n_sbp = -(-n_sb // LANES) * LANES
    cc = np.arange(n_blk)[:, None]
    ss = np.arange(n_sbp)[None, :]
    ov = jnp.asarray((ss == cc // 4).astype(np.float32) + (ss == (cc + 1) // 4).astype(np.float32), BF16)
    o_cmp_s, selmask_s = _sample_cmp(rel_bias, qs, ck_s, cv_s, ov, past, t_s, nc_s)
    o_sel_s = _sample_sel(page_table, rel_bias, qs, selmask_s, new_rows(C_KS), new_rows(C_VS),
                          pages(cache_sel_k), pages(cache_sel_v), past, t_s)
    wk = cache_win_k.shape[2]
    cwk = cache_win_k[0].reshape(nb_s, wk, KV_W)
    cwv = cache_win_v[0].reshape(nb_s, wk, KV_W)
    o_win_s = _sample_win(rel_bias, qs, new_rows(C_KW), new_rows(C_VW), cwk, cwv, past, t_s)
    o_dn_s, dn_state_s = _deltanet(proj3s, state_dn_conv[0], state_dn[0], *dn_args)
    x1_s, hn_s = dense_tail(xs2, _heads_to_rows(o_cmp_s, nb_s, t_s), _heads_to_rows(o_sel_s, nb_s, t_s),
                            _heads_to_rows(o_win_s, nb_s, t_s), o_dn_s.reshape(-1, DN_W), proj_s)
    pre = state_ffn_conv[0]
    zero = jnp.zeros((nb_s, t_s - 1, 2 * dff), F32)
    bnd1 = jnp.concatenate([pre[:, 1:2], zero], axis=1).reshape(nb_s * t_s, 2 * dff)
    bnd2 = jnp.concatenate([pre, zero[:, 1:]], axis=1).reshape(nb_s * t_s, 2 * dff)
    act_s, up_g, up_v = _ffn_up_rows(hn_s, w_up_bf, ffn_conv_w[0], bnd1, bnd2, t_s)
    y_s = _ffn_down(act_s, w_down_bf, x1_s, _row2(norm_post_ffn[0])).reshape(nb_s, t_s, d)

    keep_s = min(WINDOW, past + t_s)
    win_tail = lambda cw, c: jnp.concatenate(
        [cw, proj3s[:, :, c:c + KV_W]], axis=1)[:, wk + t_s - keep_s:].reshape(1, nb_s, keep_s, N_KV, HEAD_DIM)
    up_rows = jnp.concatenate([up_g, up_v], axis=-1).reshape(nb_s, t_s, 2 * dff)
    ffn_tail = jnp.concatenate([pre, up_rows], axis=1)[:, t_s:]
    dn_tail = jnp.concatenate([state_dn_conv[0], proj3s[:, :, C_DQ:C_DQ + 3 * DN_W]], axis=1)[:, t_s:]
    sample_state = (
        kv_rows(proj3s, C_KC), kv_rows(proj3s, C_VC), kv_rows(proj3s, C_KS), kv_rows(proj3s, C_VS),
        win_tail(cwk, C_KW), win_tail(cwv, C_VW),
        dn_state_s[None], dn_tail[None], ffn_tail[None],
    )
    return (y_p, y_s) + prompt_state + sample_state
```

```python
import functools
import math

import numpy as np
import jax
import jax.numpy as jnp
from jax import lax
from jax.experimental import pallas as pl
from jax.experimental.pallas import tpu as pltpu

F32 = jnp.float32
BF16 = jnp.bfloat16
HI = lax.Precision.HIGHEST

LANES = 128
SUBLANES = 8
VMEM_LIMIT = 56 * 1024 * 1024

HEAD_DIM = 128
N_KV = 2
GROUP = 4
N_HEADS = N_KV * GROUP
ATT_W = N_HEADS * HEAD_DIM
KV_W = N_KV * HEAD_DIM
N_DN = 8
DN_W = N_DN * HEAD_DIM
L_CMP = 32
CMP_STRIDE = 16
SEL_BLOCK = 64
N_SEL = 16
WINDOW = 512
Q_BLOCK = 128
NUM_BUCKETS = 32
DN_CONV = 4
DN_CHUNK = 64
FFN_CONV = 3
PAGE = 128
RMS_EPS = 1e-6
FORCE_SCORE = 1e4
NEG = -1e30
ATT_SCALE = HEAD_DIM ** -0.5

C_DQ, C_DK, C_DV = 0, DN_W, 2 * DN_W
C_Z = 3 * DN_W
C_Q = 4 * DN_W
C_KC = C_Q + ATT_W
C_VC, C_KS, C_VS, C_KW, C_VW = (C_KC + KV_W * i for i in range(1, 6))
C_SM = C_KC + 6 * KV_W
SM_GATE, SM_A, SM_B = 0, 3 * N_HEADS, 3 * N_HEADS + N_DN
PROJ_TN = 512
NP_COLS = -(-(C_SM + LANES) // PROJ_TN) * PROJ_TN
BF_W = C_SM - C_Q
B_Q, B_KC, B_VC, B_KS, B_VS, B_KW, B_VW = (c - C_Q for c in (C_Q, C_KC, C_VC, C_KS, C_VS, C_KW, C_VW))

CK_FRONT = 120
CK_BACK = 8
CT_OFF = CMP_STRIDE * CK_FRONT - (L_CMP - 1)


def _bucket_lower_bounds():
    n = np.arange(0, 1024)
    nf = np.maximum(n, 1).astype(np.float32)
    exact = NUM_BUCKETS // 2
    large = exact + (np.log(nf / np.float32(exact)) / np.float32(math.log(128 / exact))
                     * np.float32(NUM_BUCKETS - exact)).astype(np.int32)
    b = np.where(n < exact, n, np.minimum(large, NUM_BUCKETS - 1))
    return tuple(int(n[b >= k].min()) for k in range(NUM_BUCKETS))


BUCKET_LO = _bucket_lower_bounds()
CONST_BUCKET_DIST = BUCKET_LO[-1]


def _cp(*sem):
    return pltpu.CompilerParams(dimension_semantics=sem, vmem_limit_bytes=VMEM_LIMIT)


def _dot(a, b, precision=None):
    return jnp.dot(a, b, preferred_element_type=F32, precision=precision)


def _dot_nt(a, b, precision=None):
    return lax.dot_general(a, b, (((1,), (1,)), ((), ())), preferred_element_type=F32, precision=precision)


def _dot_tn(a, b, precision=None):
    return lax.dot_general(a, b, (((0,), (0,)), ((), ())), preferred_element_type=F32, precision=precision)


def _dot_split(a, b_bf):
    hi = a.astype(BF16)
    lo = (a - hi.astype(F32)).astype(BF16)
    return _dot(hi, b_bf) + _dot(lo, b_bf)


def _iota(shape, dim):
    return lax.broadcasted_iota(jnp.int32, shape, dim)


def _silu(x):
    return x * (1.0 / (1.0 + jnp.exp(-x)))


def _sigmoid(x):
    return 1.0 / (1.0 + jnp.exp(-x))


def _rms(x, w):
    return x * lax.rsqrt(jnp.mean(x * x, axis=-1, keepdims=True) + RMS_EPS) * w


def _rel_bias(dist, rb_of):
    out = jnp.zeros(dist.shape, F32) + rb_of(0)
    for k in range(1, NUM_BUCKETS):
        out = jnp.where(dist >= BUCKET_LO[k], rb_of(k), out)
    return out


def _inproj_kernel(x_ref, nw_ref, w_ref, o_ref, ob_ref, hn_ref, *, jb0, jb1):
    j = pl.program_id(1)

    @pl.when(j == 0)
    def _():
        hn_ref[...] = _rms(x_ref[...], nw_ref[...]).astype(BF16)

    acc = _dot(hn_ref[...], w_ref[...])
    o_ref[...] = acc

    @pl.when((j >= jb0) & (j < jb1))
    def _():
        ob_ref[...] = acc.astype(BF16)


def _in_proj(x2d, nw, w_bf):
    r, d = x2d.shape
    tm = min(512, r)
    tn = PROJ_TN
    jb0, nb = C_Q // tn, BF_W // tn
    return pl.pallas_call(
        functools.partial(_inproj_kernel, jb0=jb0, jb1=jb0 + nb),
        grid=(r // tm, NP_COLS // tn),
        in_specs=[pl.BlockSpec((tm, d), lambda i, j: (i, 0)),
                  pl.BlockSpec((1, d), lambda i, j: (0, 0)),
                  pl.BlockSpec((d, tn), lambda i, j: (0, j))],
        out_specs=[pl.BlockSpec((tm, tn), lambda i, j: (i, j)),
                   pl.BlockSpec((tm, tn), lambda i, j: (i, jnp.clip(j - jb0, 0, nb - 1)))],
        out_shape=[jax.ShapeDtypeStruct((r, NP_COLS), F32), jax.ShapeDtypeStruct((r, BF_W), BF16)],
        scratch_shapes=[pltpu.VMEM((tm, d), BF16)],
        compiler_params=_cp("parallel", "arbitrary"),
        name="in_proj",
    )(x2d, nw, w_bf)


def _cmp_seg_kernel(*refs, n_src, segs, interleaved):
    n_rows = n_src if interleaved else N_KV * n_src
    row_refs, w_ref, o_ref = refs[-(n_rows + 2):-2], refs[-2], refs[-1]
    for g in range(N_KV):
        parts = []
        for k in range(n_src):
            if interleaved:
                r = row_refs[k]
                cols = [r[pl.ds(N_KV * j + g, segs, stride=N_KV * CMP_STRIDE), :] for j in range(CMP_STRIDE)]
            else:
                r = row_refs[g * n_src + k]
                cols = [r[pl.ds(j, segs, stride=CMP_STRIDE), :] for j in range(CMP_STRIDE)]
            parts.append(jnp.concatenate(cols, axis=1))
        a = parts[0] if n_src == 1 else jnp.concatenate(parts, axis=0)
        o_ref[:, g * 2 * HEAD_DIM:(g + 1) * 2 * HEAD_DIM] = _dot(a.astype(BF16), w_ref[...])


def _cmp_seg_prompt(proj3, col, w_cat):
    n, t, _ = proj3.shape
    rows = min(2048, t)
    segs = rows // CMP_STRIDE
    return pl.pallas_call(
        functools.partial(_cmp_seg_kernel, n_src=1, segs=segs, interleaved=False),
        grid=(n, t // rows),
        in_specs=[pl.BlockSpec((None, rows, HEAD_DIM), lambda i, c, g=g: (i, c, col // HEAD_DIM + g))
                  for g in range(N_KV)]
                 + [pl.BlockSpec(w_cat.shape, lambda i, c: (0, 0))],
        out_specs=pl.BlockSpec((None, segs, 2 * KV_W), lambda i, c: (i, c, 0)),
        out_shape=jax.ShapeDtypeStruct((n, t // CMP_STRIDE, 2 * KV_W), F32),
        compiler_params=_cp("parallel", "parallel"),
        name="cmp_seg_prompt",
    )(proj3, proj3, w_cat)


PAGES_PER_STEP = 16
PAGE_ROWS = PAGE * N_KV


def _page_rows(cache):
    return cache.reshape(-1, HEAD_DIM)


def _cmp_seg_pages(cache_rows, page_table, w_cat):
    ns, n_pages = page_table.shape
    segs = PAGE // CMP_STRIDE

    def page_spec(k):
        return pl.BlockSpec((PAGE_ROWS, HEAD_DIM), lambda i, s, pt: (pt[i, s * PAGES_PER_STEP + k], 0))

    grid_spec = pltpu.PrefetchScalarGridSpec(
        num_scalar_prefetch=1,
        grid=(ns, n_pages // PAGES_PER_STEP),
        in_specs=[page_spec(k) for k in range(PAGES_PER_STEP)]
                 + [pl.BlockSpec(w_cat.shape, lambda i, s, pt: (0, 0))],
        out_specs=pl.BlockSpec((None, PAGES_PER_STEP * segs, 2 * KV_W), lambda i, s, pt: (i, s, 0)),
    )
    return pl.pallas_call(
        functools.partial(_cmp_seg_kernel, n_src=PAGES_PER_STEP, segs=segs, interleaved=True),
        grid_spec=grid_spec,
        out_shape=jax.ShapeDtypeStruct((ns, n_pages * segs, 2 * KV_W), F32),
        compiler_params=_cp("parallel", "parallel"),
        name="cmp_seg_pages",
    )(page_table, *([cache_rows] * PAGES_PER_STEP), w_cat)


def _cmp_mlp_kernel(p_ref, pe_ref, w1_ref, b1_ref, w2_ref, o_ref, *, nseg, front, back):
    p = p_ref[...]
    p0 = p[:, :HEAD_DIM]
    p1 = pltpu.roll(p[:, HEAD_DIM:], nseg - 1, 0)
    bias = b1_ref[...]
    for r in range(L_CMP // CMP_STRIDE):
        pe = jnp.broadcast_to(pe_ref[r:r + 1, :], (SUBLANES, pe_ref.shape[1]))
        bias = bias + _dot(pe, w1_ref[r], HI)[0:1, :]
    h = p0 + p1 + bias
    y = _dot(_silu(h).astype(BF16), w2_ref[...])
    y = jnp.where(_iota(y.shape, 0) < nseg - 1, y, 0.0)
    if front:
        o_ref[0:front, :] = jnp.zeros((front, HEAD_DIM), F32)
    o_ref[front:front + nseg, :] = y
    if back:
        o_ref[front + nseg:front + nseg + back, :] = jnp.zeros((back, HEAD_DIM), F32)


def _cmp_mlp(p, pe2, w1f, b1, w2_bf, front, back):
    n, nseg, _ = p.shape
    rows = front + nseg + back
    return pl.pallas_call(
        functools.partial(_cmp_mlp_kernel, nseg=nseg, front=front, back=back),
        grid=(n, N_KV),
        in_specs=[pl.BlockSpec((None, nseg, 2 * HEAD_DIM), lambda i, g: (i, 0, g)),
                  pl.BlockSpec(pe2.shape, lambda i, g: (0, 0)),
                  pl.BlockSpec(w1f.shape, lambda i, g: (0, 0, 0)),
                  pl.BlockSpec(b1.shape, lambda i, g: (0, 0)),
                  pl.BlockSpec(w2_bf.shape, lambda i, g: (0, 0))],
        out_specs=pl.BlockSpec((None, None, rows, HEAD_DIM), lambda i, g: (i, g, 0, 0)),
        out_shape=jax.ShapeDtypeStruct((n, N_KV, rows, HEAD_DIM), F32),
        compiler_params=_cp("parallel", "parallel"),
        name="cmp_mlp",
    )(p, pe2, w1f, b1, w2_bf)


def _bias_tiles_kernel(rb_ref, bt_ref, ct_ref):
    h = pl.program_id(0)
    shape = (Q_BLOCK, Q_BLOCK)
    j, i = _iota(shape, 0), _iota(shape, 1)
    last = rb_ref[NUM_BUCKETS - 1, h]
    rb_of = lambda k: rb_ref[k, h] - last
    for m in range(2):
        bt_ref[m] = _rel_bias(i - j + Q_BLOCK * m, rb_of)
    ct_ref[...] = _rel_bias(i - CMP_STRIDE * j + CT_OFF, rb_of)


def _bias_tiles(rel_bias):
    cols = GROUP * Q_BLOCK
    return pl.pallas_call(
        _bias_tiles_kernel,
        grid=(N_HEADS,),
        in_specs=[pl.BlockSpec(memory_space=pltpu.SMEM)],
        out_specs=[pl.BlockSpec((None, 2, Q_BLOCK, Q_BLOCK), lambda h: (h // GROUP, 0, 0, h % GROUP)),
                   pl.BlockSpec((None, Q_BLOCK, Q_BLOCK), lambda h: (h // GROUP, 0, h % GROUP))],
        out_shape=[jax.ShapeDtypeStruct((N_KV, 2, Q_BLOCK, cols), F32),
                   jax.ShapeDtypeStruct((N_KV, Q_BLOCK, cols), F32)],
        compiler_params=_cp("parallel"),
        name="bias_tiles",
    )(rel_bias)


def _online_update_t(x, weigh_values, m_ref, l_ref, acc_ref, valid=None):
    m_old = m_ref[...]
    m_new = jnp.maximum(m_old, jnp.max(x, axis=0, keepdims=True))
    alpha = jnp.exp(m_old - m_new)
    e = jnp.exp(x - m_new)
    if valid is not None:
        e = jnp.where(valid, e, 0.0)
    l_ref[...] = alpha * l_ref[...] + jnp.sum(e, axis=0, keepdims=True)
    acc_ref[...] = alpha * acc_ref[...] + weigh_values(e.astype(BF16))
    m_ref[...] = m_new
    return e, alpha


def _online_update(s, v_bf, m_ref, l_ref, acc_ref, valid=None):
    m_old = m_ref[...]
    m_new = jnp.maximum(m_old, jnp.max(s, axis=1, keepdims=True))
    alpha = jnp.exp(m_old - m_new)
    e = jnp.exp(s - m_new)
    if valid is not None:
        e = jnp.where(valid, e, 0.0)
    l_ref[...] = alpha * l_ref[...] + jnp.sum(e, axis=1, keepdims=True)
    acc_ref[...] = alpha * acc_ref[...] + _dot(e.astype(BF16), v_bf)
    m_ref[...] = m_new
    return e, alpha


def _top_k_mask(score, axis):
    pos = _iota(score.shape, axis)
    sel = jnp.zeros(score.shape, F32)
    for _ in range(N_SEL):
        mx = jnp.max(score, axis=axis, keepdims=True)
        idx = jnp.min(jnp.where(score == mx, pos, score.shape[axis]), axis=axis, keepdims=True)
        hit = pos == idx
        sel = jnp.where(hit, 1.0, sel)
        score = jnp.where(hit, -jnp.inf, score)
    return sel


def _block_scores(imp, qpos, blk):
    cur = qpos >> 6
    causal = blk * SEL_BLOCK <= qpos
    forced = (blk == 0) | (blk == cur) | (blk == cur - 1)
    return jnp.where(causal, imp + jnp.where(forced, FORCE_SCORE, 0.0), NEG)


def _overlap_tile_t(c0, shape):
    s = _iota(shape, 0)
    c = c0 + _iota(shape, 1)
    return ((s == (c >> 2)).astype(F32) + (s == ((c + 1) >> 2)).astype(F32)).astype(BF16)


def _attn_init(m_ref, l_ref, acc_ref):
    m_ref[...] = jnp.full(m_ref.shape, NEG, F32)
    l_ref[...] = jnp.zeros(l_ref.shape, F32)
    acc_ref[...] = jnp.zeros(acc_ref.shape, F32)


def _write_heads_t(o_ref, ot):
    for r in range(GROUP):
        o_ref[:, r * HEAD_DIM:(r + 1) * HEAD_DIM] = ot[:, r * Q_BLOCK:(r + 1) * Q_BLOCK].T


def _lanes4(x):
    return jnp.concatenate([x] * GROUP, axis=1)


def _pcmp_kernel(qt_ref, ck_ref, cv_ref, ct_ref, o_ref, ns_ref, m_ref, l_ref, acc_ref, imp_ref, *, n_sb):
    b = pl.program_id(2)
    cols = GROUP * Q_BLOCK
    qt = qt_ref[...]
    _attn_init(m_ref, l_ref, acc_ref)
    imp_ref[...] = jnp.zeros((n_sb, cols), F32)
    n_old_blocks = SUBLANES * b - CK_FRONT

    def step(c0, row0, bias, valid):
        kt = ck_ref[pl.ds(row0, Q_BLOCK), :].astype(BF16)
        vt = cv_ref[pl.ds(row0, Q_BLOCK), :].astype(BF16)
        x = _dot(kt, qt) * ATT_SCALE
        if bias is not None:
            x = x + bias
        x = jnp.where(valid, x, NEG)
        e, alpha = _online_update_t(x, lambda eb: _dot_tn(vt, eb), m_ref, l_ref, acc_ref, valid)
        hi = e.astype(BF16)
        lo = (e - hi.astype(F32)).astype(BF16)
        ovt = _overlap_tile_t(c0, (n_sb, Q_BLOCK))
        imp_ref[...] = alpha * imp_ref[...] + _dot(ovt, hi) + _dot(ovt, lo)

    def old_tile(t, carry):
        c0 = t * Q_BLOCK
        valid = (c0 + _iota((Q_BLOCK, cols), 0)) < n_old_blocks
        step(c0, pl.multiple_of(CK_FRONT + c0, SUBLANES), None, valid)
        return carry

    lax.fori_loop(0, (jnp.maximum(n_old_blocks, 0) + Q_BLOCK - 1) // Q_BLOCK, old_tile, 0)

    shape = (Q_BLOCK, cols)
    j = _iota(shape, 0)
    i = _iota(shape, 1) & (Q_BLOCK - 1)
    valid = (i - CMP_STRIDE * j + CT_OFF >= 0) & (j >= -n_old_blocks)
    step(n_old_blocks, pl.multiple_of(SUBLANES * b, SUBLANES), ct_ref[...], valid)

    inv_l = 1.0 / jnp.maximum(l_ref[...], 1e-30)
    _write_heads_t(o_ref, acc_ref[...] * inv_l)
    pimp = imp_ref[...] * inv_l
    imp = pimp[:, 0:Q_BLOCK]
    for r in range(1, GROUP):
        imp = imp + pimp[:, r * Q_BLOCK:(r + 1) * Q_BLOCK]
    blk = _iota((n_sb, Q_BLOCK), 0)
    qpos = b * Q_BLOCK + _iota((n_sb, Q_BLOCK), 1)
    ns_ref[...] = (1.0 - _top_k_mask(_block_scores(imp, qpos, blk), 0)).astype(BF16)


def _prompt_cmp(qt, ckp, cvp, ctt):
    n, _, nqb, _, cols = qt.shape
    t = nqb * Q_BLOCK
    n_sb = t // SEL_BLOCK
    ck_rows = ckp.shape[2]
    return pl.pallas_call(
        functools.partial(_pcmp_kernel, n_sb=n_sb),
        grid=(n, N_KV, nqb),
        in_specs=[pl.BlockSpec((None, None, None, HEAD_DIM, cols), lambda i, g, b: (i, g, b, 0, 0)),
                  pl.BlockSpec((None, None, ck_rows, HEAD_DIM), lambda i, g, b: (i, g, 0, 0)),
                  pl.BlockSpec((None, None, ck_rows, HEAD_DIM), lambda i, g, b: (i, g, 0, 0)),
                  pl.BlockSpec((None, Q_BLOCK, cols), lambda i, g, b: (g, 0, 0))],
        out_specs=[pl.BlockSpec((None, Q_BLOCK, GROUP * HEAD_DIM), lambda i, g, b: (i, b, g)),
                   pl.BlockSpec((None, None, None, n_sb, Q_BLOCK), lambda i, g, b: (i, g, b, 0, 0))],
        out_shape=[jax.ShapeDtypeStruct((n, t, ATT_W), F32),
                   jax.ShapeDtypeStruct((n, N_KV, nqb, n_sb, Q_BLOCK), BF16)],
        scratch_shapes=[pltpu.VMEM((1, cols), F32), pltpu.VMEM((1, cols), F32),
                        pltpu.VMEM((HEAD_DIM, cols), F32), pltpu.VMEM((n_sb, cols), F32)],
        compiler_params=_cp("parallel", "parallel", "arbitrary"),
        name="prompt_cmp",
    )(qt, ckp, cvp, ctt)


MASK_BIG = 2.0 ** 100


def _attend_tiles(k_ref, vt_ref, rhs, stats, t0, nt, add, n_sb=None):
    keys = nt * Q_BLOCK
    lhs = k_ref[pl.ds(pl.multiple_of(t0 * Q_BLOCK, Q_BLOCK), keys), :]
    if n_sb is not None:
        blk_of_key = (Q_BLOCK // SEL_BLOCK) * t0 + (_iota((keys, n_sb), 0) >> 6)
        penalty = jnp.where(_iota((keys, n_sb), 1) == blk_of_key, -MASK_BIG, 0.0).astype(BF16)
        lhs = jnp.concatenate([lhs, penalty], axis=1)
    x = _dot(lhs, rhs) * ATT_SCALE
    if any(a is not None for a in add):
        rows = [x[k * Q_BLOCK:(k + 1) * Q_BLOCK] for k in range(nt)]
        rows = [r if a is None else r + a for r, a in zip(rows, add)]
        x = rows[0] if nt == 1 else jnp.concatenate(rows, axis=0)
    vt = vt_ref[t0] if nt == 1 else jnp.concatenate([vt_ref[t0 + k] for k in range(nt)], axis=1)
    _online_update_t(x, lambda eb: _dot(vt, eb), *stats)


def _psel_kernel(qt_ref, k_ref, vt_ref, ns_ref, bt_ref, o_ref, m_ref, l_ref, acc_ref):
    b = pl.program_id(2)
    n_sb = ns_ref.shape[0]
    stats = (m_ref, l_ref, acc_ref)
    _attn_init(*stats)
    rhs = jnp.concatenate([qt_ref[...], _lanes4(ns_ref[...])], axis=0)
    tile = (Q_BLOCK, Q_BLOCK)
    causal = _lanes4(jnp.where(_iota(tile, 0) <= _iota(tile, 1), 0.0, NEG))
    attend = functools.partial(_attend_tiles, k_ref, vt_ref, rhs, stats, n_sb=n_sb)

    @pl.when(b == 0)
    def _():
        attend(0, 1, [bt_ref[0] + causal])

    @pl.when(b >= 1)
    def _():
        attend(b - 1, 2, [bt_ref[1], bt_ref[0] + causal])

    n_far = jnp.maximum(b - 1, 0)

    def far_pair(p, carry):
        attend(2 * p, 2, [None, None])
        return carry

    lax.fori_loop(0, n_far // 2, far_pair, 0)

    @pl.when(n_far % 2 == 1)
    def _():
        attend(n_far - 1, 1, [None])

    _write_heads_t(o_ref, acc_ref[...] / jnp.maximum(l_ref[...], 1e-30))


def _pwin_kernel(qt_ref, k_ref, vt_ref, bt_ref, o_ref, m_ref, l_ref, acc_ref):
    b = pl.program_id(2)
    stats = (m_ref, l_ref, acc_ref)
    _attn_init(*stats)
    tile = (Q_BLOCK, Q_BLOCK)
    key, qry = _iota(tile, 0), _iota(tile, 1)
    causal = _lanes4(jnp.where(key <= qry, 0.0, NEG))
    in_window = _lanes4(jnp.where(key > qry, 0.0, NEG))
    attend = functools.partial(_attend_tiles, k_ref, vt_ref, qt_ref[...], stats)
    n_back = WINDOW // Q_BLOCK

    @pl.when(b >= n_back)
    def _():
        attend(b - n_back, n_back + 1,
               [in_window] + [None] * (n_back - 2) + [bt_ref[1], bt_ref[0] + causal])

    @pl.when(b < n_back)
    def _():
        attend(b, 1, [bt_ref[0] + causal])
        for back in range(1, n_back):
            @pl.when(b >= back)
            def _(back=back):
                attend(b - back, 1, [bt_ref[1] if back == 1 else None])

    _write_heads_t(o_ref, acc_ref[...] / jnp.maximum(l_ref[...], 1e-30))


def _prompt_attend(kernel_fn, name, qt, pbf3, k_col, vt, btt, notsel=None):
    n, _, nqb, _, cols = qt.shape
    t = nqb * Q_BLOCK
    by_block = lambda i, g, b: (i, g, b, 0, 0)
    specs = [pl.BlockSpec((None, None, None, HEAD_DIM, cols), by_block),
             pl.BlockSpec((None, t, HEAD_DIM), lambda i, g, b: (i, 0, k_col // HEAD_DIM + g)),
             pl.BlockSpec((None, None, nqb, HEAD_DIM, Q_BLOCK), lambda i, g, b: (i, g, 0, 0, 0))]
    args = [qt, pbf3, vt]
    if notsel is not None:
        specs.append(pl.BlockSpec((None, None, None) + notsel.shape[3:], by_block))
        args.append(notsel)
    specs.append(pl.BlockSpec((None, 2, Q_BLOCK, cols), lambda i, g, b: (g, 0, 0, 0)))
    args.append(btt)
    return pl.pallas_call(
        kernel_fn,
        grid=(n, N_KV, nqb),
        in_specs=specs,
        out_specs=pl.BlockSpec((None, Q_BLOCK, GROUP * HEAD_DIM), lambda i, g, b: (i, b, g)),
        out_shape=jax.ShapeDtypeStruct((n, t, ATT_W), F32),
        scratch_shapes=[pltpu.VMEM((1, cols), F32), pltpu.VMEM((1, cols), F32), pltpu.VMEM((HEAD_DIM, cols), F32)],
        compiler_params=_cp("parallel", "parallel", "arbitrary"),
        name=name,
    )(*args)


def _sample_rows(t_s):
    rows = GROUP * t_s
    ridx = _iota((rows, 1), 0)
    return rows, ridx % t_s, ridx // t_s


def _rb_col(rb_ref, g, r_of_row):
    def rb_of(k):
        col = jnp.zeros(r_of_row.shape, F32)
        for r in range(GROUP):
            col = jnp.where(r_of_row == r, rb_ref[k, g * GROUP + r], col)
        return col
    return rb_of


def _new_key_scores(qf, kn, q_idx, rb_of):
    t_s = kn.shape[0]
    cols = []
    for j in range(t_s):
        s = jnp.sum(qf * kn[j:j + 1, :], axis=1, keepdims=True) * ATT_SCALE
        cols.append(s + _rel_bias(q_idx - j, rb_of))
    return cols


def _scmp_kernel(rb_ref, q_ref, ck_ref, cv_ref, ov_ref, o_ref, sel_ref, *, past, t_s, nc):
    rows, q_idx, r_of_row = _sample_rows(t_s)
    qpos = past + q_idx
    n_blk = ck_ref.shape[1]
    for g in range(N_KV):
        rb_of = _rb_col(rb_ref, g, r_of_row)
        q = q_ref[g]
        c = _iota((rows, n_blk), 1)
        dist = qpos - (c * CMP_STRIDE + (L_CMP - 1))
        valid = (dist >= 0) & (c < nc)
        s = _dot_nt(q, ck_ref[g].astype(BF16)) * ATT_SCALE + _rel_bias(dist, rb_of)
        s = jnp.where(valid, s, NEG)
        m = jnp.max(s, axis=1, keepdims=True)
        p = jnp.where(valid, jnp.exp(s - m), 0.0)
        p = p / jnp.maximum(jnp.sum(p, axis=1, keepdims=True), 1e-30)
        o_ref[g] = _dot(p.astype(BF16), cv_ref[g].astype(BF16))
        imp = _dot_split(p, ov_ref[...])
        tot = imp
        for r in range(1, GROUP):
            tot = tot + pltpu.roll(imp, r * t_s, 0)
        sel_ref[g] = _top_k_mask(_block_scores(tot, qpos, _iota(tot.shape, 1)), 1)


def _sample_cmp(rel_bias, qs, ck, cv, ov, past, t_s, nc):
    ns = qs.shape[0]
    rows = GROUP * t_s
    n_blk = ck.shape[2]
    n_sbp = ov.shape[1]
    return pl.pallas_call(
        functools.partial(_scmp_kernel, past=past, t_s=t_s, nc=nc),
        grid=(ns,),
        in_specs=[pl.BlockSpec(memory_space=pltpu.SMEM),
                  pl.BlockSpec((None, N_KV, rows, HEAD_DIM), lambda i: (i, 0, 0, 0)),
                  pl.BlockSpec((None, N_KV, n_blk, HEAD_DIM), lambda i: (i, 0, 0, 0)),
                  pl.BlockSpec((None, N_KV, n_blk, HEAD_DIM), lambda i: (i, 0, 0, 0)),
                  pl.BlockSpec(ov.shape, lambda i: (0, 0))],
        out_specs=[pl.BlockSpec((None, N_KV, rows, HEAD_DIM), lambda i: (i, 0, 0, 0)),
                   pl.BlockSpec((None, N_KV, rows, n_sbp), lambda i: (i, 0, 0, 0))],
        out_shape=[jax.ShapeDtypeStruct((ns, N_KV, rows, HEAD_DIM), F32),
                   jax.ShapeDtypeStruct((ns, N_KV, rows, n_sbp), F32)],
        compiler_params=_cp("parallel"),
        name="sample_cmp",
    )(rel_bias, qs, ck, cv, ov)


def _ssel_kernel(pt_ref, rb_ref, q_ref, sel_ref, kn_ref, vn_ref, *refs, past, t_s):
    k_pages = refs[:PAGES_PER_STEP]
    v_pages = refs[PAGES_PER_STEP:2 * PAGES_PER_STEP]
    o_ref, m_ref, l_ref, acc_ref, s_ref = refs[2 * PAGES_PER_STEP:]
    step = pl.program_id(1)
    last = pl.num_programs(1) - 1
    rows, q_idx, r_of_row = _sample_rows(t_s)
    qpos = past + q_idx
    keys = PAGES_PER_STEP * PAGE
    n_sbp = sel_ref.shape[2]

    @pl.when(step == 0)
    def _():
        for g in range(N_KV):
            rb_of = _rb_col(rb_ref, g, r_of_row)
            qf = q_ref[g].astype(F32)
            kn, vn = kn_ref[g], vn_ref[g]
            selm = sel_ref[g]
            cols = _new_key_scores(qf, kn, q_idx, rb_of)
            valid = []
            for j in range(t_s):
                blk = (past + j) // SEL_BLOCK
                valid.append((selm[:, blk:blk + 1] > 0.5) & (j <= q_idx))
            m = jnp.full((rows, 1), NEG, F32)
            for j in range(t_s):
                m = jnp.maximum(m, jnp.where(valid[j], cols[j], NEG))
            l = jnp.zeros((rows, 1), F32)
            acc = jnp.zeros((rows, HEAD_DIM), F32)
            for j in range(t_s):
                e = jnp.where(valid[j], jnp.exp(cols[j] - m), 0.0)
                l = l + e
                acc = acc + e * vn[j:j + 1, :]
            m_ref[g], l_ref[g], acc_ref[g] = m, l, acc

    @pl.when(step >= 1)
    def _():
        grp = step - 1
        tok = grp * keys + _iota((rows, keys), 1)
        dist = qpos - tok
        blocks_per_step = keys // SEL_BLOCK
        expand = (_iota((n_sbp, keys), 0) == blocks_per_step * grp + (_iota((n_sbp, keys), 1) >> 6)).astype(BF16)
        for g in range(N_KV):
            rb_of = _rb_col(rb_ref, g, r_of_row)
            group_rows = pl.ds(g, PAGE, stride=N_KV)
            kt = jnp.concatenate([p[group_rows, :] for p in k_pages], axis=0).astype(BF16)
            vt = jnp.concatenate([p[group_rows, :] for p in v_pages], axis=0).astype(BF16)
            raw = _dot_nt(q_ref[g], kt) * ATT_SCALE
            s_ref[...] = raw + rb_of(NUM_BUCKETS - 1)

            @pl.when(step == last)
            def _():
                s_ref[...] = raw + _rel_bias(dist, rb_of)

            picked = (_dot(sel_ref[g].astype(BF16), expand) > 0.5) & (dist >= 0)
            s = s_ref[...] + jnp.where(picked, 0.0, NEG)
            _online_update(s, vt, m_ref.at[g], l_ref.at[g], acc_ref.at[g])

    @pl.when(step == last)
    def _():
        for g in range(N_KV):
            o_ref[g] = acc_ref[g] / jnp.maximum(l_ref[g], 1e-30)


def _sample_sel(page_table, rel_bias, qs, selmask, ksn, vsn, cache_k, cache_v, past, t_s):
    ns, n_pages = page_table.shape
    rows = GROUP * t_s
    n_sbp = selmask.shape[-1]
    n_grp = n_pages // PAGES_PER_STEP

    def page_spec(k):
        return pl.BlockSpec((PAGE_ROWS, HEAD_DIM),
                            lambda i, s, pt: (pt[i, jnp.maximum(s - 1, 0) * PAGES_PER_STEP + k], 0))

    whole = lambda i, s, pt: (i, 0, 0, 0)
    grid_spec = pltpu.PrefetchScalarGridSpec(
        num_scalar_prefetch=1,
        grid=(ns, n_grp + 1),
        in_specs=[pl.BlockSpec(memory_space=pltpu.SMEM),
                  pl.BlockSpec((None, N_KV, rows, HEAD_DIM), whole),
                  pl.BlockSpec((None, N_KV, rows, n_sbp), whole),
                  pl.BlockSpec((None, N_KV, t_s, HEAD_DIM), whole),
                  pl.BlockSpec((None, N_KV, t_s, HEAD_DIM), whole)]
                 + [page_spec(k) for k in range(PAGES_PER_STEP)] * 2,
        out_specs=pl.BlockSpec((None, N_KV, rows, HEAD_DIM), whole),
        scratch_shapes=[pltpu.VMEM((N_KV, rows, 1), F32), pltpu.VMEM((N_KV, rows, 1), F32),
                        pltpu.VMEM((N_KV, rows, HEAD_DIM), F32),
                        pltpu.VMEM((rows, PAGES_PER_STEP * PAGE), F32)],
    )
    return pl.pallas_call(
        functools.partial(_ssel_kernel, past=past, t_s=t_s),
        grid_spec=grid_spec,
        out_shape=jax.ShapeDtypeStruct((ns, N_KV, rows, HEAD_DIM), F32),
        compiler_params=_cp("parallel", "arbitrary"),
        name="sample_sel",
    )(page_table, rel_bias, qs, selmask, ksn, vsn, *([cache_k] * PAGES_PER_STEP), *([cache_v] * PAGES_PER_STEP))


def _swin_kernel(rb_ref, q_ref, kn_ref, vn_ref, ck_ref, cv_ref, o_ref, *, past, t_s):
    rows, q_idx, r_of_row = _sample_rows(t_s)
    wk = ck_ref.shape[0] // N_KV
    kpos = past - wk + _iota((rows, wk), 1)
    dist = past + q_idx - kpos
    valid = (dist >= 0) & (dist < WINDOW) & (kpos >= 0)
    for g in range(N_KV):
        rb_of = _rb_col(rb_ref, g, r_of_row)
        group_rows = pl.ds(g, wk, stride=N_KV)
        kt = ck_ref[group_rows, :].astype(BF16)
        vt = cv_ref[group_rows, :].astype(BF16)
        s = _dot_nt(q_ref[g], kt) * ATT_SCALE + _rel_bias(dist, rb_of)
        s = jnp.where(valid, s, NEG)
        kn, vn = kn_ref[g], vn_ref[g]
        new = _new_key_scores(q_ref[g].astype(F32), kn, q_idx, rb_of)
        new_ok = [j <= q_idx for j in range(t_s)]
        m = jnp.max(s, axis=1, keepdims=True)
        for j in range(t_s):
            m = jnp.maximum(m, jnp.where(new_ok[j], new[j], NEG))
        e = jnp.where(valid, jnp.exp(s - m), 0.0)
        l = jnp.sum(e, axis=1, keepdims=True)
        acc = _dot(e.astype(BF16), vt)
        for j in range(t_s):
            ej = jnp.where(new_ok[j], jnp.exp(new[j] - m), 0.0)
            l = l + ej
            acc = acc + ej * vn[j:j + 1, :]
        o_ref[g] = acc / jnp.maximum(l, 1e-30)


def _sample_win(rel_bias, qs, kwn, vwn, cwk, cwv, past, t_s):
    ns = qs.shape[0]
    rows = GROUP * t_s
    wk_rows = cwk.shape[1]
    whole = lambda i: (i, 0, 0, 0)
    return pl.pallas_call(
        functools.partial(_swin_kernel, past=past, t_s=t_s),
        grid=(ns,),
        in_specs=[pl.BlockSpec(memory_space=pltpu.SMEM),
                  pl.BlockSpec((None, N_KV, rows, HEAD_DIM), whole),
                  pl.BlockSpec((None, N_KV, t_s, HEAD_DIM), whole),
                  pl.BlockSpec((None, N_KV, t_s, HEAD_DIM), whole),
                  pl.BlockSpec((None, wk_rows, HEAD_DIM), lambda i: (i, 0, 0)),
                  pl.BlockSpec((None, wk_rows, HEAD_DIM), lambda i: (i, 0, 0))],
        out_specs=pl.BlockSpec((None, N_KV, rows, HEAD_DIM), whole),
        out_shape=jax.ShapeDtypeStruct((ns, N_KV, rows, HEAD_DIM), F32),
        compiler_params=_cp("parallel"),
        name="sample_win",
    )(rel_bias, qs, kwn, vwn, cwk, cwv)


EXT_TOP = 8


def _dn_kernel(xq_ref, xk_ref, xv_ref, z_ref, sm_ref, cprev_ref, s0_ref, cw_ref, alog_ref, dtb_ref, nw_ref,
               o_ref, sout_ref, ext_ref, st_ref, *, tc):
    c = pl.program_id(1)
    ch = DN_CHUNK
    keep = DN_CONV - 1

    @pl.when(c == 0)
    def _():
        ext_ref[EXT_TOP - keep:EXT_TOP, :] = cprev_ref[...]
        st_ref[...] = s0_ref[...]

    @pl.when(c > 0)
    def _():
        ext_ref[EXT_TOP - keep:EXT_TOP, :] = ext_ref[EXT_TOP + ch - keep:EXT_TOP + ch, :]

    ext_ref[EXT_TOP:EXT_TOP + tc, 0:DN_W] = xq_ref[...]
    ext_ref[EXT_TOP:EXT_TOP + tc, DN_W:2 * DN_W] = xk_ref[...]
    ext_ref[EXT_TOP:EXT_TOP + tc, 2 * DN_W:3 * DN_W] = xv_ref[...]
    if tc < ch:
        ext_ref[EXT_TOP + tc:EXT_TOP + ch, :] = jnp.zeros((ch - tc, 3 * DN_W), F32)

    y = ext_ref[EXT_TOP - keep:EXT_TOP - keep + ch, :] * cw_ref[0:1, :]
    for j in range(1, DN_CONV):
        y = y + ext_ref[EXT_TOP - keep + j:EXT_TOP - keep + j + ch, :] * cw_ref[j:j + 1, :]
    y = _silu(y)

    sq = (ch, ch)
    ri, ci = _iota(sq, 0), _iota(sq, 1)
    tril = ci <= ri
    strict = ci < ri
    tril_f = tril.astype(F32)
    triu_f = (ri <= ci).astype(F32)
    ones_f = jnp.ones(sq, F32)
    eye_f = (ri == ci).astype(F32)

    sm = sm_ref[...]
    a_all = sm[:, SM_A:SM_A + N_DN]
    b_all = sm[:, SM_B:SM_B + N_DN]
    xs = a_all + dtb_ref[...]
    softplus = jnp.maximum(xs, 0.0) + jnp.log(1.0 + jnp.exp(-jnp.abs(xs)))
    g_all = -jnp.exp(alog_ref[...]) * softplus
    beta_all = _sigmoid(b_all)
    if tc < ch:
        live = _iota((ch, 1), 0) < tc
        pad = jnp.zeros((ch - tc, N_DN), F32)
        g_all = jnp.concatenate([g_all, pad], axis=0)
        beta_all = jnp.concatenate([beta_all, pad], axis=0)
        y = jnp.where(live, y, 0.0)

    for h in range(N_DN):
        hs = slice(h * HEAD_DIM, (h + 1) * HEAD_DIM)
        q = y[:, h * HEAD_DIM:(h + 1) * HEAD_DIM]
        k = y[:, DN_W + h * HEAD_DIM:DN_W + (h + 1) * HEAD_DIM]
        v = y[:, 2 * DN_W + h * HEAD_DIM:2 * DN_W + (h + 1) * HEAD_DIM]
        q = q * lax.rsqrt(jnp.sum(q * q, axis=-1, keepdims=True) + RMS_EPS) * (HEAD_DIM ** -0.5)
        k = k * lax.rsqrt(jnp.sum(k * k, axis=-1, keepdims=True) + RMS_EPS)
        gb = jnp.broadcast_to(g_all[:, h:h + 1], sq)
        beta = beta_all[:, h:h + 1]
        gc_rows = _dot(tril_f, gb, HI)
        gc_cols = _dot(ones_f, gb * triu_f, HI)
        decay = jnp.exp(jnp.where(tril, gc_rows - gc_cols, NEG))
        gc = gc_rows[:, 0:1]
        gc_last = gc_rows[ch - 1:ch, 0:1]
        egc = jnp.exp(gc)
        kb = k * beta
        a_mat = jnp.where(strict, _dot_nt(kb, k, HI) * decay, 0.0)
        pw = -a_mat
        inv = eye_f + pw
        for _ in range(int(math.log2(ch)) - 1):
            pw = _dot(pw, pw, HI)
            inv = inv + _dot(inv, pw, HI)
        sol = _dot(inv, jnp.concatenate([v * beta, kb * egc], axis=1), HI)
        u, w = sol[:, :HEAD_DIM], sol[:, HEAD_DIM:]
        attn = jnp.where(tril, _dot_nt(q, k, HI) * decay, 0.0)
        s_h = st_ref[h]
        v_new = u - _dot(w, s_h, HI)
        o = _dot(q * egc, s_h, HI) + _dot(attn, v_new, HI)
        st_ref[h] = s_h * jnp.exp(gc_last) + _dot_tn(k * jnp.exp(gc_last - gc), v_new, HI)
        o = o[0:tc]
        o = o * lax.rsqrt(jnp.mean(o * o, axis=-1, keepdims=True) + RMS_EPS) * nw_ref[...]
        o_ref[:, hs] = (o * _silu(z_ref[:, hs])).astype(o_ref.dtype)

    @pl.when(c == pl.num_programs(1) - 1)
    def _():
        sout_ref[...] = st_ref[...]


def _deltanet(proj3, conv_prev, s0, conv_w, a_log, dt_bias, norm_w):
    n, t, _ = proj3.shape
    tc = min(DN_CHUNK, t)
    assert t % tc == 0 and t >= DN_CONV - 1
    nck = t // tc
    col = lambda cb: (lambda i, c: (i, c, cb))
    const2 = lambda i, c: (0, 0)
    return pl.pallas_call(
        functools.partial(_dn_kernel, tc=tc),
        grid=(n, nck),
        in_specs=[pl.BlockSpec((None, tc, DN_W), col(C_DQ // DN_W)),
                  pl.BlockSpec((None, tc, DN_W), col(C_DK // DN_W)),
                  pl.BlockSpec((None, tc, DN_W), col(C_DV // DN_W)),
                  pl.BlockSpec((None, tc, DN_W), col(C_Z // DN_W)),
                  pl.BlockSpec((None, tc, LANES), col(C_SM // LANES)),
                  pl.BlockSpec((None, DN_CONV - 1, 3 * DN_W), lambda i, c: (i, 0, 0)),
                  pl.BlockSpec((None, N_DN, HEAD_DIM, HEAD_DIM), lambda i, c: (i, 0, 0, 0)),
                  pl.BlockSpec(conv_w.shape, const2),
                  pl.BlockSpec(a_log.shape, const2),
                  pl.BlockSpec(dt_bias.shape, const2),
                  pl.BlockSpec(norm_w.shape, const2)],
        out_specs=[pl.BlockSpec((None, tc, DN_W), lambda i, c: (i, c, 0)),
                   pl.BlockSpec((None, N_DN, HEAD_DIM, HEAD_DIM), lambda i, c: (i, 0, 0, 0))],
        out_shape=[jax.ShapeDtypeStruct((n, t, DN_W), BF16),
                   jax.ShapeDtypeStruct((n, N_DN, HEAD_DIM, HEAD_DIM), F32)],
        scratch_shapes=[pltpu.VMEM((EXT_TOP + DN_CHUNK, 3 * DN_W), F32),
                        pltpu.VMEM((N_DN, HEAD_DIM, HEAD_DIM), F32)],
        compiler_params=_cp("parallel", "arbitrary"),
        name="deltanet",
    )(proj3, proj3, proj3, proj3, proj3, conv_prev, s0, conv_w, a_log, dt_bias, norm_w)


def _outproj_kernel(x_ref, oc_ref, os_ref, ow_ref, od_ref, sm_ref, w_ref, npost_ref, npre_ref,
                    x1_ref, hn_ref, mix_ref):
    gates = _sigmoid(sm_ref[:, SM_GATE:SM_GATE + 3 * N_HEADS])
    for h in range(N_HEADS):
        hs = slice(h * HEAD_DIM, (h + 1) * HEAD_DIM)
        o = (gates[:, 3 * h:3 * h + 1] * oc_ref[:, hs] + gates[:, 3 * h + 1:3 * h + 2] * os_ref[:, hs]
             + gates[:, 3 * h + 2:3 * h + 3] * ow_ref[:, hs])
        mix_ref[:, hs] = o.astype(BF16)
    mix_ref[:, ATT_W:] = od_ref[...]
    mix = _dot(mix_ref[...], w_ref[...])
    x1 = x_ref[...] + _rms(mix, npost_ref[...])
    x1_ref[...] = x1
    hn_ref[...] = _rms(x1, npre_ref[...]).astype(BF16)


def _out_proj(x2d, o_cmp, o_sel, o_win, o_dn, proj2d, w_out_bf, n_post, n_pre):
    r, d = x2d.shape
    tm = min(256, r)
    row = lambda i: (i, 0)
    const = lambda i: (0, 0)
    return pl.pallas_call(
        _outproj_kernel,
        grid=(r // tm,),
        in_specs=[pl.BlockSpec((tm, d), row),
                  pl.BlockSpec((tm, ATT_W), row), pl.BlockSpec((tm, ATT_W), row), pl.BlockSpec((tm, ATT_W), row),
                  pl.BlockSpec((tm, DN_W), row),
                  pl.BlockSpec((tm, LANES), lambda i: (i, C_SM // LANES)),
                  pl.BlockSpec(w_out_bf.shape, const),
                  pl.BlockSpec((1, d), const), pl.BlockSpec((1, d), const)],
        out_specs=[pl.BlockSpec((tm, d), row), pl.BlockSpec((tm, d), row)],
        out_shape=[jax.ShapeDtypeStruct((r, d), F32), jax.ShapeDtypeStruct((r, d), BF16)],
        scratch_shapes=[pltpu.VMEM((tm, ATT_W + DN_W), BF16)],
        compiler_params=_cp("parallel"),
        name="out_proj",
    )(x2d, o_cmp, o_sel, o_win, o_dn, proj2d, w_out_bf, n_post, n_pre)


def _ffn_conv_gate(extg_ref, extv_ref, cwg_ref, cwv_ref, tm, fix=None):
    keep = FFN_CONV - 1
    outs = []
    for ext_ref, cw_ref, idx in ((extg_ref, cwg_ref, 0), (extv_ref, cwv_ref, 1)):
        y = ext_ref[EXT_TOP:EXT_TOP + tm, :] * cw_ref[keep:keep + 1, :]
        for back in range(1, keep + 1):
            shifted = ext_ref[EXT_TOP - back:EXT_TOP - back + tm, :]
            if fix is not None:
                shifted = fix(shifted, back, idx)
            y = y + shifted * cw_ref[keep - back:keep - back + 1, :]
        outs.append(y)
    return _silu(outs[0]) * outs[1]


def _ffn_up_seq_kernel(hn_ref, wg_ref, wv_ref, cwg_ref, cwv_ref, pg_ref, pv_ref, act_ref, lg_ref, lv_ref,
                       extg_ref, extv_ref, *, tm, tiles_per_seq):
    i = pl.program_id(1)
    keep = FFN_CONV - 1
    first = i % tiles_per_seq == 0
    for ext_ref, p_ref in ((extg_ref, pg_ref), (extv_ref, pv_ref)):
        @pl.when(first)
        def _(ext_ref=ext_ref, p_ref=p_ref):
            ext_ref[EXT_TOP - keep:EXT_TOP, :] = p_ref[...]

        @pl.when(jnp.logical_not(first))
        def _(ext_ref=ext_ref):
            ext_ref[EXT_TOP - keep:EXT_TOP, :] = ext_ref[EXT_TOP + tm - keep:EXT_TOP + tm, :]

    hn = hn_ref[...]
    extg_ref[EXT_TOP:EXT_TOP + tm, :] = _dot(hn, wg_ref[...])
    extv_ref[EXT_TOP:EXT_TOP + tm, :] = _dot(hn, wv_ref[...])
    act_ref[...] = _ffn_conv_gate(extg_ref, extv_ref, cwg_ref, cwv_ref, tm).astype(BF16)

    @pl.when(i % tiles_per_seq == tiles_per_seq - 1)
    def _():
        lg_ref[...] = extg_ref[EXT_TOP + tm - keep:EXT_TOP + tm, :]
        lv_ref[...] = extv_ref[EXT_TOP + tm - keep:EXT_TOP + tm, :]


def _ffn_up_seq(hn, w_up_bf, conv_w, prev, n_seq, t):
    r, d = hn.shape
    dff = w_up_bf.shape[1] // 2
    tm = min(512, t)
    tn = 512
    nj = dff // tn
    tiles_per_seq = t // tm
    keep = FFN_CONV - 1
    return pl.pallas_call(
        functools.partial(_ffn_up_seq_kernel, tm=tm, tiles_per_seq=tiles_per_seq),
        grid=(nj, r // tm),
        in_specs=[pl.BlockSpec((tm, d), lambda j, i: (i, 0)),
                  pl.BlockSpec((d, tn), lambda j, i: (0, j)),
                  pl.BlockSpec((d, tn), lambda j, i: (0, nj + j)),
                  pl.BlockSpec((FFN_CONV, tn), lambda j, i: (0, j)),
                  pl.BlockSpec((FFN_CONV, tn), lambda j, i: (0, nj + j)),
                  pl.BlockSpec((None, keep, tn), lambda j, i: (i // tiles_per_seq, 0, j)),
                  pl.BlockSpec((None, keep, tn), lambda j, i: (i // tiles_per_seq, 0, nj + j))],
        out_specs=[pl.BlockSpec((tm, tn), lambda j, i: (i, j)),
                   pl.BlockSpec((None, keep, tn), lambda j, i: (i // tiles_per_seq, 0, j)),
                   pl.BlockSpec((None, keep, tn), lambda j, i: (i // tiles_per_seq, 0, j))],
        out_shape=[jax.ShapeDtypeStruct((r, dff), BF16),
                   jax.ShapeDtypeStruct((n_seq, keep, dff), F32),
                   jax.ShapeDtypeStruct((n_seq, keep, dff), F32)],
        scratch_shapes=[pltpu.VMEM((EXT_TOP + tm, tn), F32), pltpu.VMEM((EXT_TOP + tm, tn), F32)],
        compiler_params=_cp("parallel", "arbitrary"),
        name="ffn_up_seq",
    )(hn, w_up_bf, w_up_bf, conv_w, conv_w, prev, prev)


def _ffn_up_rows_kernel(hn_ref, wg_ref, wv_ref, cwg_ref, cwv_ref, bg1_ref, bg2_ref, bv1_ref, bv2_ref,
                        act_ref, ug_ref, uv_ref, extg_ref, extv_ref, *, tm, t):
    hn = hn_ref[...]
    ug = _dot(hn, wg_ref[...])
    uv = _dot(hn, wv_ref[...])
    ug_ref[...] = ug
    uv_ref[...] = uv
    extg_ref[0:EXT_TOP, :] = jnp.zeros((EXT_TOP, ug.shape[1]), F32)
    extv_ref[0:EXT_TOP, :] = jnp.zeros((EXT_TOP, uv.shape[1]), F32)
    extg_ref[EXT_TOP:EXT_TOP + tm, :] = ug
    extv_ref[EXT_TOP:EXT_TOP + tm, :] = uv
    pos = _iota((tm, 1), 0) % t
    bnd = ((bg1_ref, bg2_ref), (bv1_ref, bv2_ref))

    def fix(shifted, back, idx):
        return jnp.where(pos < back, bnd[idx][back - 1][...], shifted)

    act_ref[...] = _ffn_conv_gate(extg_ref, extv_ref, cwg_ref, cwv_ref, tm, fix).astype(BF16)


def _ffn_up_rows(hn, w_up_bf, conv_w, bnd1, bnd2, t):
    r, d = hn.shape
    dff = w_up_bf.shape[1] // 2
    tn = 512
    nj = dff // tn
    lo = lambda j: (0, j)
    hi = lambda j: (0, nj + j)
    return pl.pallas_call(
        functools.partial(_ffn_up_rows_kernel, tm=r, t=t),
        grid=(nj,),
        in_specs=[pl.BlockSpec((r, d), lambda j: (0, 0)),
                  pl.BlockSpec((d, tn), lo), pl.BlockSpec((d, tn), hi),
                  pl.BlockSpec((FFN_CONV, tn), lo), pl.BlockSpec((FFN_CONV, tn), hi),
                  pl.BlockSpec((r, tn), lo), pl.BlockSpec((r, tn), lo),
                  pl.BlockSpec((r, tn), hi), pl.BlockSpec((r, tn), hi)],
        out_specs=[pl.BlockSpec((r, tn), lo), pl.BlockSpec((r, tn), lo), pl.BlockSpec((r, tn), lo)],
        out_shape=[jax.ShapeDtypeStruct((r, dff), BF16),
                   jax.ShapeDtypeStruct((r, dff), F32), jax.ShapeDtypeStruct((r, dff), F32)],
        scratch_shapes=[pltpu.VMEM((EXT_TOP + r, tn), F32), pltpu.VMEM((EXT_TOP + r, tn), F32)],
        compiler_params=_cp("parallel"),
        name="ffn_up_rows",
    )(hn, w_up_bf, w_up_bf, conv_w, conv_w, bnd1, bnd2, bnd1, bnd2)


def _ffn_down_kernel(a_ref, w_ref, x_ref, nw_ref, y_ref, acc_ref):
    k = pl.program_id(1)

    @pl.when(k == 0)
    def _():
        acc_ref[...] = jnp.zeros(acc_ref.shape, F32)

    acc_ref[...] += _dot(a_ref[...], w_ref[...])

    @pl.when(k == pl.num_programs(1) - 1)
    def _():
        y_ref[...] = x_ref[...] + _rms(acc_ref[...], nw_ref[...])


def _ffn_down(act, w_down_bf, x1, nw):
    r, dff = act.shape
    d = x1.shape[1]
    tm = min(512, r)
    tk = 512
    return pl.pallas_call(
        _ffn_down_kernel,
        grid=(r // tm, dff // tk),
        in_specs=[pl.BlockSpec((tm, tk), lambda i, k: (i, k)),
                  pl.BlockSpec((tk, d), lambda i, k: (k, 0)),
                  pl.BlockSpec((tm, d), lambda i, k: (i, 0)),
                  pl.BlockSpec((1, d), lambda i, k: (0, 0))],
        out_specs=pl.BlockSpec((tm, d), lambda i, k: (i, 0)),
        out_shape=jax.ShapeDtypeStruct((r, d), F32),
        scratch_shapes=[pltpu.VMEM((tm, d), F32)],
        compiler_params=_cp("parallel", "arbitrary"),
        name="ffn_down",
    )(act, w_down_bf, x1, nw)


def _permute_w_in(w_in):
    o_q, o_kv = 0, ATT_W
    o_gate = o_kv + 6 * KV_W
    o_qkv = o_gate + 3 * N_HEADS
    o_z = o_qkv + 3 * DN_W
    o_a = o_z + DN_W
    o_b = o_a + N_DN
    perm = np.concatenate([np.arange(o_qkv, o_z), np.arange(o_z, o_a), np.arange(o_q, o_gate),
                           np.arange(o_gate, o_qkv), np.arange(o_a, o_b + N_DN)])
    w = jnp.take(w_in, jnp.asarray(perm, jnp.int32), axis=1)
    return jnp.pad(w, ((0, 0), (0, NP_COLS - perm.size))).astype(BF16)


def _cmp_weights(w1, b1, w2, pe):
    d = HEAD_DIM
    ratio = L_CMP // CMP_STRIDE
    w1f = w1.reshape(ratio, CMP_STRIDE * d, w1.shape[-1])
    w_cat = jnp.concatenate([w1f[r] for r in range(ratio)], axis=1).astype(BF16)
    return w_cat, pe.reshape(ratio, CMP_STRIDE * d), w1f, b1.reshape(1, -1), w2.astype(BF16)


def _row2(v):
    return v.reshape(1, -1)


def _heads_to_rows(o, ns, t_s):
    o = o.reshape(ns, N_KV, GROUP, t_s, HEAD_DIM)
    return jnp.transpose(o, (0, 3, 1, 2, 4)).reshape(ns * t_s, ATT_W)


def kernel(x_prompt, x_sample, cache_cmp_k, cache_cmp_v, cache_sel_k, cache_sel_v, cache_win_k, cache_win_v,
           state_dn, state_dn_conv, state_ffn_conv, page_table, rel_bias, w_in, w_out,
           norm_pre_mix, norm_post_mix, norm_pre_ffn, norm_post_ffn,
           cmp_w1_k, cmp_b1_k, cmp_w2_k, cmp_pe_k, cmp_w1_v, cmp_b1_v, cmp_w2_v, cmp_pe_v,
           dn_conv_w, dn_a_log, dn_dt_bias, dn_norm_w, ffn_w_up, ffn_conv_w, ffn_w_down):
    assert w_in.shape[0] == 1, "single layer"
    nb_p, t_p, d = x_prompt.shape
    nb_s, t_s, _ = x_sample.shape
    n_pages = page_table.shape[1]
    past = n_pages * PAGE
    dff = ffn_w_down.shape[1]
    assert t_p % 2048 == 0 or t_p in (256, 512, 1024)
    assert n_pages % PAGES_PER_STEP == 0

    w_in_bf = _permute_w_in(w_in[0])
    w_out_bf = w_out[0].astype(BF16)
    w_up_bf = ffn_w_up[0].astype(BF16)
    w_down_bf = ffn_w_down[0].astype(BF16)
    cw_k = _cmp_weights(cmp_w1_k[0], cmp_b1_k[0], cmp_w2_k[0], cmp_pe_k[0])
    cw_v = _cmp_weights(cmp_w1_v[0], cmp_b1_v[0], cmp_w2_v[0], cmp_pe_v[0])
    bt, ct = _bias_tiles(rel_bias)
    dn_args = (dn_conv_w[0], _row2(dn_a_log[0]), _row2(dn_dt_bias[0]), _row2(dn_norm_w[0]))

    def dense_tail(x2d, o_cmp, o_sel, o_win, o_dn, proj):
        return _out_proj(x2d, o_cmp, o_sel, o_win, o_dn, proj, w_out_bf,
                         _row2(norm_post_mix[0]), _row2(norm_pre_ffn[0]))

    xp2 = x_prompt.reshape(nb_p * t_p, d)
    proj_p, pbf_p = _in_proj(xp2, _row2(norm_pre_mix[0]), w_in_bf)
    proj3 = proj_p.reshape(nb_p, t_p, NP_COLS)
    pbf3 = pbf_p.reshape(nb_p, t_p, BF_W)
    ck_p = _cmp_mlp(_cmp_seg_prompt(proj3, C_KC, cw_k[0]), *cw_k[1:], CK_FRONT, CK_BACK)
    cv_p = _cmp_mlp(_cmp_seg_prompt(proj3, C_VC, cw_v[0]), *cw_v[1:], CK_FRONT, CK_BACK)
    nqb = t_p // Q_BLOCK
    qt = jnp.transpose(pbf3[:, :, B_Q:B_Q + ATT_W].reshape(nb_p, nqb, Q_BLOCK, N_KV, GROUP, HEAD_DIM),
                       (0, 3, 1, 5, 4, 2)).reshape(nb_p, N_KV, nqb, HEAD_DIM, GROUP * Q_BLOCK)
    values_t = lambda c: jnp.transpose(
        pbf3[:, :, c:c + KV_W].reshape(nb_p, nqb, Q_BLOCK, N_KV, HEAD_DIM), (0, 3, 1, 4, 2))
    o_cmp_p, notsel_p = _prompt_cmp(qt, ck_p, cv_p, ct)
    o_sel_p = _prompt_attend(_psel_kernel, "prompt_sel", qt, pbf3, B_KS, values_t(B_VS), bt, notsel_p)
    o_win_p = _prompt_attend(_pwin_kernel, "prompt_win", qt, pbf3, B_KW, values_t(B_VW), bt)
    o_dn_p, dn_state_p = _deltanet(
        proj3, jnp.zeros((nb_p, DN_CONV - 1, 3 * DN_W), F32),
        jnp.zeros((nb_p, N_DN, HEAD_DIM, HEAD_DIM), F32), *dn_args)
    x1_p, hn_p = dense_tail(xp2, o_cmp_p.reshape(-1, ATT_W), o_sel_p.reshape(-1, ATT_W),
                            o_win_p.reshape(-1, ATT_W), o_dn_p.reshape(-1, DN_W), proj_p)
    act_p, ffc_g, ffc_v = _ffn_up_seq(hn_p, w_up_bf, ffn_conv_w[0],
                                      jnp.zeros((nb_p, FFN_CONV - 1, 2 * dff), F32), nb_p, t_p)
    y_p = _ffn_down(act_p, w_down_bf, x1_p, _row2(norm_post_ffn[0])).reshape(nb_p, t_p, d)

    kv_rows = lambda p3, c: p3[:, :, c:c + KV_W].reshape(1, p3.shape[0], p3.shape[1], N_KV, HEAD_DIM)
    keep_p = min(WINDOW, t_p)
    prompt_state = (
        kv_rows(proj3, C_KC), kv_rows(proj3, C_VC), kv_rows(proj3, C_KS), kv_rows(proj3, C_VS),
        kv_rows(proj3, C_KW)[:, :, t_p - keep_p:], kv_rows(proj3, C_VW)[:, :, t_p - keep_p:],
        dn_state_p[None],
        proj3[:, t_p - (DN_CONV - 1):, C_DQ:C_DQ + 3 * DN_W][None],
        jnp.concatenate([ffc_g, ffc_v], axis=-1)[None],
    )

    xs2 = x_sample.reshape(nb_s * t_s, d)
    proj_s, pbf_s = _in_proj(xs2, _row2(norm_pre_mix[0]), w_in_bf)
    proj3s = proj_s.reshape(nb_s, t_s, NP_COLS)
    rows = GROUP * t_s
    qs = jnp.transpose(pbf_s[:, B_Q:B_Q + ATT_W].reshape(nb_s, t_s, N_KV, GROUP, HEAD_DIM),
                       (0, 2, 3, 1, 4)).reshape(nb_s, N_KV, rows, HEAD_DIM)
    new_rows = lambda c: jnp.transpose(proj3s[:, :, c:c + KV_W].reshape(nb_s, t_s, N_KV, HEAD_DIM), (0, 2, 1, 3))
    pages = _page_rows

    ck_s = _cmp_mlp(_cmp_seg_pages(pages(cache_cmp_k), page_table, cw_k[0]), *cw_k[1:], 0, 0)
    cv_s = _cmp_mlp(_cmp_seg_pages(pages(cache_cmp_v), page_table, cw_v[0]), *cw_v[1:], 0, 0)
    seq_len = past + t_s
    nc_s = (seq_len - L_CMP) // CMP_STRIDE + 1
    n_blk = ck_s.shape[2]
    n_sb = -(-seq_len // SEL_BLOCK)
    n_sbp = -(-n_sb // LANES) * LANES
    cc = np.arange(n_blk)[:, None]
    ss = np.arange(n_sbp)[None, :]
    ov = jnp.asarray((ss == cc // 4).astype(np.float32) + (ss == (cc + 1) // 4).astype(np.float32), BF16)
    o_cmp_s, selmask_s = _sample_cmp(rel_bias, qs, ck_s, cv_s, ov, past, t_s, nc_s)
    o_sel_s = _sample_sel(page_table, rel_bias, qs, selmask_s, new_rows(C_KS), new_rows(C_VS),
                          pages(cache_sel_k), pages(cache_sel_v), past, t_s)
    wk = cache_win_k.shape[2]
    cwk = cache_win_k.reshape(nb_s, wk * N_KV, HEAD_DIM)
    cwv = cache_win_v.reshape(nb_s, wk * N_KV, HEAD_DIM)
    o_win_s = _sample_win(rel_bias, qs, new_rows(C_KW), new_rows(C_VW), cwk, cwv, past, t_s)
    o_dn_s, dn_state_s = _deltanet(proj3s, state_dn_conv[0], state_dn[0], *dn_args)
    x1_s, hn_s = dense_tail(xs2, _heads_to_rows(o_cmp_s, nb_s, t_s), _heads_to_rows(o_sel_s, nb_s, t_s),
                            _heads_to_rows(o_win_s, nb_s, t_s), o_dn_s.reshape(-1, DN_W), proj_s)
    pre = state_ffn_conv[0]
    zero = jnp.zeros((nb_s, t_s - 1, 2 * dff), F32)
    bnd1 = jnp.concatenate([pre[:, 1:2], zero], axis=1).reshape(nb_s * t_s, 2 * dff)
    bnd2 = jnp.concatenate([pre, zero[:, 1:]], axis=1).reshape(nb_s * t_s, 2 * dff)
    act_s, up_g, up_v = _ffn_up_rows(hn_s, w_up_bf, ffn_conv_w[0], bnd1, bnd2, t_s)
    y_s = _ffn_down(act_s, w_down_bf, x1_s, _row2(norm_post_ffn[0])).reshape(nb_s, t_s, d)

    keep_s = min(WINDOW, past + t_s)
    win_tail = lambda cache, c: jnp.concatenate(
        [cache[0], proj3s[:, :, c:c + KV_W].reshape(nb_s, t_s, N_KV, HEAD_DIM)], axis=1)[None, :, wk + t_s - keep_s:]
    up_rows = jnp.concatenate([up_g, up_v], axis=-1).reshape(nb_s, t_s, 2 * dff)
    ffn_tail = jnp.concatenate([pre, up_rows], axis=1)[:, t_s:]
    dn_tail = jnp.concatenate([state_dn_conv[0], proj3s[:, :, C_DQ:C_DQ + 3 * DN_W]], axis=1)[:, t_s:]
    sample_state = (
        kv_rows(proj3s, C_KC), kv_rows(proj3s, C_VC), kv_rows(proj3s, C_KS), kv_rows(proj3s, C_VS),
        win_tail(cache_win_k, C_KW), win_tail(cache_win_v, C_VW),
        dn_state_s[None], dn_tail[None], ffn_tail[None],
    )
    return (y_p, y_s) + prompt_state + sample_state
```
